```python
import math
import jax, jax.numpy as jnp
from jax import lax
import numpy as np

D_MODEL = 1024
BATCH = 8
SEQ = 2048
DEPTH = 4

MEM_LEN = 256
DIFF_HEADS = 8
DIFF_HEAD_DIM = 64
DIFF_WIDTH = DIFF_HEADS * 2 * DIFF_HEAD_DIM
Q_BLOCK = 128
RET_HEADS = 4
RET_KEY_DIM = 128
RET_VAL_DIM = 256
RET_QK_WIDTH = RET_HEADS * RET_KEY_DIM
RET_WIDTH = RET_HEADS * RET_VAL_DIM
RET_CHUNK = 128
LRU_WIDTH = D_MODEL
LRU_BLOCKS = 8
LRU_BLOCK_W = LRU_WIDTH // LRU_BLOCKS
CONV_WIDTH = 4
LRU_C = 8.0
N_BRANCH = 3
XA_HEADS = 4
XA_HEAD_DIM = 128
XA_WIDTH = XA_HEADS * XA_HEAD_DIM
D_FF = 4 * D_MODEL
IN_WIDTHS = (DIFF_WIDTH, DIFF_WIDTH, DIFF_WIDTH, RET_QK_WIDTH, RET_QK_WIDTH, RET_WIDTH, RET_WIDTH, LRU_WIDTH, LRU_WIDTH, N_BRANCH * D_MODEL)
IN_COLS = 3 * DIFF_WIDTH + 2 * RET_QK_WIDTH + 2 * RET_WIDTH + 2 * LRU_WIDTH + N_BRANCH * D_MODEL

kernel_name = 'hybrid_diffattn_retention_rglru_gated'


def _split_points():
    pts, acc = [], 0
    for w in IN_WIDTHS[:-1]:
        acc += w
        pts.append(acc)
    return pts


def rmsnorm(x, g, eps=1e-6):
    xf = x.astype(jnp.float32)
    y = xf * lax.rsqrt(jnp.mean(xf * xf, axis=-1, keepdims=True) + eps)
    return (y * g.astype(jnp.float32)).astype(x.dtype)


def group_norm_heads(o, eps=1e-5):
    mu = jnp.mean(o, axis=-1, keepdims=True)
    var = jnp.mean(jnp.square(o - mu), axis=-1, keepdims=True)
    return (o - mu) * lax.rsqrt(var + eps)


def rotate_every_two(t):
    t1 = t[..., 0::2]
    t2 = t[..., 1::2]
    return jnp.stack((-t2, t1), axis=-1).reshape(t.shape)


def diff_attention(q, k, v, lam):
    bn, s_len = q.shape[0], q.shape[1]
    nb = s_len // Q_BLOCK
    qb = q.reshape(bn, nb, Q_BLOCK, DIFF_HEADS, 2, DIFF_HEAD_DIM).swapaxes(0, 1)
    starts = jnp.arange(nb, dtype=jnp.int32) * Q_BLOCK
    kpos = jnp.arange(s_len, dtype=jnp.int32)
    scale = DIFF_HEAD_DIM ** -0.5

    def block(args):
        q_blk, start = args
        s = jnp.einsum('bqhcd,bkhcd->bhcqk', q_blk, k).astype(jnp.float32) * scale
        qpos = start + jnp.arange(Q_BLOCK, dtype=jnp.int32)
        causal = kpos[None, :] <= qpos[:, None]
        s = jnp.where(causal, s, -jnp.inf)
        p = jax.nn.softmax(s, axis=-1)
        a = p[:, :, 0] - lam * p[:, :, 1]
        return jnp.einsum('bhqk,bkhe->bqhe', a.astype(v.dtype), v)

    o = lax.map(block, (qb, starts))
    return o.swapaxes(0, 1).reshape(bn, s_len, DIFF_HEADS, 2 * DIFF_HEAD_DIM)


def retention_chunkwise(q, k, v, log_g):
    bn, s_len = q.shape[0], q.shape[1]
    n_chunks = s_len // RET_CHUNK

    def chunk(t):
        return t.reshape(bn, n_chunks, RET_CHUNK, RET_HEADS, t.shape[-1]).transpose(0, 3, 1, 2, 4)

    q, k, v = chunk(q), chunk(k), chunk(v)
    idx = jnp.arange(RET_CHUNK, dtype=jnp.float32)
    rel = idx[:, None] - idx[None, :]
    decay = jnp.where(rel[None] >= 0, jnp.exp(jnp.maximum(rel, 0.0)[None] * log_g[:, None, None]), 0.0)
    s = jnp.einsum('bhncd,bhnmd->bhncm', q, k) * decay[:, None]
    o_intra = jnp.einsum('bhncm,bhnme->bhnce', s, v)
    w_k = jnp.exp((RET_CHUNK - 1 - idx)[None, :] * log_g[:, None])
    kv = jnp.einsum('bhncd,hc,bhnce->bhnde', k, w_k, v)
    chunk_decay = jnp.exp(RET_CHUNK * log_g)[None, :, None, None]

    def step(state, kv_n):
        return chunk_decay * state + kv_n, state

    init = jnp.zeros((bn, RET_HEADS, RET_KEY_DIM, RET_VAL_DIM), jnp.float32)
    _, prev = lax.scan(step, init, jnp.moveaxis(kv, 2, 0))
    prev = jnp.moveaxis(prev, 0, 2)
    w_q = jnp.exp((idx + 1.0)[None, :] * log_g[:, None])
    o_cross = jnp.einsum('bhncd,bhnde->bhnce', q, prev) * w_q[None, :, None, :, None]
    o = o_intra + o_cross
    return o.transpose(0, 2, 3, 1, 4).reshape(bn, s_len, RET_HEADS, RET_VAL_DIM)


def rg_lru_branch(xb, conv_w, conv_b, w_a, b_a, w_x, b_x, lam):
    bn, s_len, width = xb.shape
    xc = lax.conv_general_dilated(xb, conv_w.astype(xb.dtype), window_strides=(1,), padding=[(CONV_WIDTH - 1, 0)], dimension_numbers=('NWC', 'WIO', 'NWC'), feature_group_count=width) + conv_b
    xg = xc.reshape(bn, s_len, LRU_BLOCKS, LRU_BLOCK_W)
    r = jax.nn.sigmoid(jnp.einsum('bsnc,ncd->bsnd', xg, w_a).reshape(bn, s_len, width) + b_a)
    i = jax.nn.sigmoid(jnp.einsum('bsnc,ncd->bsnd', xg, w_x).reshape(bn, s_len, width) + b_x)
    log_a = -LRU_C * r.astype(jnp.float32) * jax.nn.softplus(-lam.astype(jnp.float32))
    a = jnp.exp(log_a)
    mult = jnp.sqrt(-jnp.expm1(2.0 * log_a))
    b = mult * (i * xc).astype(jnp.float32)

    def combine(lhs, rhs):
        a1, b1 = lhs
        a2, b2 = rhs
        return a1 * a2, a2 * b1 + b2

    _, h = lax.associative_scan(combine, (a, b), axis=1)
    return h.astype(xb.dtype)


def setup_inputs(seed: int = 0) -> dict:
    key = jax.random.key(seed)
    ks = jax.random.split(key, 32)
    f32 = jnp.float32

    def nrm(k, shape, fan_in):
        return jax.random.normal(k, shape, f32) * fan_in ** -0.5

    def gain(k, shape):
        return 1.0 + 0.01 * jax.random.normal(k, shape, f32)

    def small(k, shape, scale=0.01):
        return scale * jax.random.normal(k, shape, f32)

    u = jax.random.uniform(ks[14], (DEPTH, LRU_WIDTH), f32, minval=0.81, maxval=0.998)
    sa = jnp.sqrt(u)
    lru_lambda = jnp.log(sa) - jnp.log1p(-sa)
    return {
        'x': jax.random.normal(ks[0], (BATCH, SEQ, D_MODEL), f32),
        'mem': jax.random.normal(ks[1], (BATCH, MEM_LEN, D_MODEL), f32),
        'norm_mix': gain(ks[2], (DEPTH, D_MODEL)),
        'w_in': nrm(ks[3], (DEPTH, D_MODEL, IN_COLS), D_MODEL),
        'diff_lq1': small(ks[4], (DEPTH, DIFF_HEAD_DIM), 0.1),
        'diff_lk1': small(ks[5], (DEPTH, DIFF_HEAD_DIM), 0.1),
        'diff_lq2': small(ks[6], (DEPTH, DIFF_HEAD_DIM), 0.1),
        'diff_lk2': small(ks[7], (DEPTH, DIFF_HEAD_DIM), 0.1),
        'diff_subln': gain(ks[8], (DEPTH, 2 * DIFF_HEAD_DIM)),
        'lru_conv_w': nrm(ks[9], (DEPTH, CONV_WIDTH, 1, LRU_WIDTH), CONV_WIDTH),
        'lru_conv_b': small(ks[10], (DEPTH, LRU_WIDTH)),
        'lru_wa': nrm(ks[11], (DEPTH, LRU_BLOCKS, LRU_BLOCK_W, LRU_BLOCK_W), LRU_BLOCK_W),
        'lru_ba': small(ks[12], (DEPTH, LRU_WIDTH)),
        'lru_wx': nrm(ks[13], (DEPTH, LRU_BLOCKS, LRU_BLOCK_W, LRU_BLOCK_W), LRU_BLOCK_W),
        'lru_bx': small(ks[15], (DEPTH, LRU_WIDTH)),
        'lru_lambda': lru_lambda,
        'w_branch': nrm(ks[16], (DEPTH, N_BRANCH, DIFF_WIDTH, D_MODEL), DIFF_WIDTH),
        'w_out': nrm(ks[17], (DEPTH, D_MODEL, D_MODEL), D_MODEL),
        'norm_xattn': gain(ks[18], (DEPTH, D_MODEL)),
        'norm_mem': gain(ks[19], (DEPTH, D_MODEL)),
        'xa_wq': nrm(ks[20], (DEPTH, D_MODEL, XA_WIDTH), D_MODEL),
        'xa_wkv': nrm(ks[21], (DEPTH, D_MODEL, 2 * XA_WIDTH), D_MODEL),
        'xa_wo': nrm(ks[22], (DEPTH, XA_WIDTH, D_MODEL), XA_WIDTH),
        'norm_mlp': gain(ks[23], (DEPTH, D_MODEL)),
        'mlp_w1': nrm(ks[24], (DEPTH, D_MODEL, D_FF), D_MODEL),
        'mlp_w2': nrm(ks[25], (DEPTH, D_FF, D_MODEL), D_FF),
        'norm_final': gain(ks[26], (D_MODEL,)),
    }


def reference(x, mem, norm_mix, w_in, diff_lq1, diff_lk1, diff_lq2, diff_lk2, diff_subln, lru_conv_w, lru_conv_b, lru_wa, lru_ba, lru_wx, lru_bx, lru_lambda, w_branch, w_out, norm_xattn, norm_mem, xa_wq, xa_wkv, xa_wo, norm_mlp, mlp_w1, mlp_w2, norm_final):
    f32 = jnp.float32
    bn, s_len, _ = x.shape
    m_len = mem.shape[1]
    splits = _split_points()
    pos = jnp.arange(s_len, dtype=f32)
    angle = jnp.repeat(1.0 / (10000.0 ** jnp.linspace(0.0, 1.0, RET_KEY_DIM // 2, dtype=f32)), 2)
    phase = pos[:, None] * angle[None, :]
    cos = jnp.cos(phase)[None, :, None, :]
    sin = jnp.sin(phase)[None, :, None, :]
    log_g = jnp.log(1.0 - jnp.exp2(-5.0 - jnp.arange(RET_HEADS, dtype=f32)))

    for l in range(DEPTH):
        h = rmsnorm(x, norm_mix[l])
        u = h @ w_in[l]
        dq, dk, dv, rq, rk, rv, rg, lx, ly, gates = jnp.split(u, splits, axis=-1)

        lam_init = 0.8 - 0.6 * math.exp(-0.3 * l)
        lam = (jnp.exp(jnp.sum(diff_lq1[l].astype(f32) * diff_lk1[l].astype(f32)))
               - jnp.exp(jnp.sum(diff_lq2[l].astype(f32) * diff_lk2[l].astype(f32))) + lam_init)
        od = diff_attention(dq.reshape(bn, s_len, DIFF_HEADS, 2, DIFF_HEAD_DIM),
                            dk.reshape(bn, s_len, DIFF_HEADS, 2, DIFF_HEAD_DIM),
                            dv.reshape(bn, s_len, DIFF_HEADS, 2 * DIFF_HEAD_DIM), lam)
        od = (rmsnorm(od, diff_subln[l], eps=1e-5) * (1.0 - lam_init)).reshape(bn, s_len, DIFF_WIDTH)

        q_r = rq.reshape(bn, s_len, RET_HEADS, RET_KEY_DIM).astype(f32)
        k_r = rk.reshape(bn, s_len, RET_HEADS, RET_KEY_DIM).astype(f32)
        q_r = q_r * cos + rotate_every_two(q_r) * sin
        k_r = (k_r * cos + rotate_every_two(k_r) * sin) * RET_KEY_DIM ** -0.5
        v_r = rv.reshape(bn, s_len, RET_HEADS, RET_VAL_DIM).astype(f32)
        o_r = group_norm_heads(retention_chunkwise(q_r, k_r, v_r, log_g)).reshape(bn, s_len, RET_WIDTH)
        o_r = (jax.nn.silu(rg.astype(f32)) * o_r).astype(x.dtype)

        hl = rg_lru_branch(lx, lru_conv_w[l], lru_conv_b[l], lru_wa[l], lru_ba[l], lru_wx[l], lru_bx[l], lru_lambda[l])
        o_l = hl * jax.nn.gelu(ly)

        g = jax.nn.sigmoid(gates.reshape(bn, s_len, N_BRANCH, D_MODEL))
        merged = (g[:, :, 0] * (od @ w_branch[l, 0])
                  + g[:, :, 1] * (o_r @ w_branch[l, 1])
                  + g[:, :, 2] * (o_l @ w_branch[l, 2]))
        x = x + merged @ w_out[l]

        hx = rmsnorm(x, norm_xattn[l])
        hm = rmsnorm(mem, norm_mem[l])
        q_x = (hx @ xa_wq[l]).reshape(bn, s_len, XA_HEADS, XA_HEAD_DIM)
        kv_m = (hm @ xa_wkv[l]).reshape(bn, m_len, 2, XA_HEADS, XA_HEAD_DIM)
        s_x = jnp.einsum('bshd,bmhd->bhsm', q_x, kv_m[:, :, 0]).astype(f32) * XA_HEAD_DIM ** -0.5
        p_x = jax.nn.softmax(s_x, axis=-1)
        o_x = jnp.einsum('bhsm,bmhd->bshd', p_x.astype(x.dtype), kv_m[:, :, 1]).reshape(bn, s_len, XA_WIDTH)
        x = x + o_x @ xa_wo[l]

        hf = rmsnorm(x, norm_mlp[l])
        x = x + jnp.square(jax.nn.relu(hf @ mlp_w1[l])) @ mlp_w2[l]

    return rmsnorm(x, norm_final)
```

```python
import functools
import math

import jax
import jax.numpy as jnp
from jax import lax
from jax.experimental import pallas as pl
from jax.experimental.pallas import tpu as pltpu

F32 = jnp.float32
BF16 = jnp.bfloat16

D_MODEL = 1024
DIFF_HEADS = 8
DIFF_HEAD_DIM = 64
DIFF_VAL_DIM = 2 * DIFF_HEAD_DIM
RET_HEADS = 4
RET_KEY_DIM = 128
RET_VAL_DIM = 256
RET_CHUNK = 128
LRU_BLOCKS = 8
LRU_BLOCK_W = 128
CONV_WIDTH = 4
LRU_C = 8.0
XA_HEADS = 4
XA_HEAD_DIM = 128
XA_WIDTH = XA_HEADS * XA_HEAD_DIM
D_FF = 4 * D_MODEL

COL_DQ, COL_DK, COL_DV = 0, 1024, 2048
COL_RQ, COL_RK, COL_RV, COL_RG = 3072, 3584, 4096, 5120
COL_LX, COL_LY, COL_GATES = 6144, 7168, 8192
IN_COLS = 11264

VMEM_LIMIT_BYTES = 56 * 1024 * 1024

ATT_TQ = 256
ATT_TK = 256
LRU_TS = 256
SUBLANES = 8


def _params(*sem):
    return pltpu.CompilerParams(dimension_semantics=sem, vmem_limit_bytes=VMEM_LIMIT_BYTES)


def _rms_rows(xf, g, eps):
    return xf * lax.rsqrt(jnp.mean(xf * xf, axis=-1, keepdims=True) + eps) * g


def _norm_matmul_kernel(x_ref, g_ref, w_ref, o_ref, h_ref):
    @pl.when(pl.program_id(1) == 0)
    def _():
        h_ref[...] = _rms_rows(x_ref[...], g_ref[...], 1e-6).astype(BF16)

    o_ref[...] = jnp.dot(h_ref[...], w_ref[...], preferred_element_type=F32).astype(o_ref.dtype)


def _norm_matmul(x2, g, w, layer, n_cols, out_dtype, tm, tn):
    m, d = x2.shape
    tm = min(tm, m)
    return pl.pallas_call(
        _norm_matmul_kernel,
        grid=(m // tm, n_cols // tn),
        in_specs=[
            pl.BlockSpec((tm, d), lambda i, j: (i, 0)),
            pl.BlockSpec((None, 1, d), lambda i, j: (layer, 0, 0)),
            pl.BlockSpec((None, d, tn), lambda i, j: (layer, 0, j)),
        ],
        out_specs=pl.BlockSpec((tm, tn), lambda i, j: (i, j)),
        out_shape=jax.ShapeDtypeStruct((m, n_cols), out_dtype),
        scratch_shapes=[pltpu.VMEM((tm, d), BF16)],
        compiler_params=_params("parallel", "arbitrary"),
        name="norm_matmul",
    )(x2, g, w)


def _diff_attn_kernel(lq1_ref, lk1_ref, lq2_ref, lk2_ref, cst_ref, subln_ref, q_ref, k_ref, v_ref,
                      o_ref, kb_ref, vt_ref, m_ref, l_ref, acc_ref):
    i = pl.program_id(2)
    tq, tk = ATT_TQ, ATT_TK
    n_kv = kb_ref.shape[0] // tk

    @pl.when(i == 0)
    def _():
        kb_ref[...] = k_ref[...].astype(BF16)
        for t in range(n_kv):
            vt_ref[t] = v_ref[t * tk:(t + 1) * tk, :].T.astype(BF16)

    q = q_ref[...] * (DIFF_HEAD_DIM ** -0.5)
    lane = lax.broadcasted_iota(jnp.int32, q.shape, 1)
    qq = jnp.concatenate([jnp.where(lane < DIFF_HEAD_DIM, q, 0.0).astype(BF16),
                          jnp.where(lane >= DIFF_HEAD_DIM, q, 0.0).astype(BF16)], axis=0)

    m_ref[...] = jnp.full(m_ref.shape, -jnp.inf, F32)
    l_ref[...] = jnp.zeros(l_ref.shape, F32)
    acc_ref[...] = jnp.zeros(acc_ref.shape, F32)

    def tile(j, masked):
        k_t = kb_ref[pl.ds(pl.multiple_of(j * tk, tk), tk), :]
        s = lax.dot_general(k_t, qq, (((1,), (1,)), ((), ())), preferred_element_type=F32)
        if masked:
            key = lax.broadcasted_iota(jnp.int32, (tk, tq), 0)
            qry = lax.broadcasted_iota(jnp.int32, (tk, tq), 1)
            ok = key <= qry
            s = jnp.where(jnp.concatenate([ok, ok], axis=1), s, -jnp.inf)
        m_old = m_ref[...]
        m_new = jnp.maximum(m_old, jnp.max(s, axis=0, keepdims=True))
        alpha = jnp.exp(m_old - m_new)
        e = jnp.exp(s - m_new)
        l_ref[...] = alpha * l_ref[...] + jnp.sum(e, axis=0, keepdims=True)
        acc_ref[...] = alpha * acc_ref[...] + jnp.dot(vt_ref[j], e.astype(BF16), preferred_element_type=F32)
        m_ref[...] = m_new

    def body(j, carry):
        tile(j, False)
        return carry

    lax.fori_loop(0, i, body, 0)
    tile(i, True)

    lam_init = cst_ref[0:1, 0:1]
    out_scale = cst_ref[0:1, 1:2]
    lam = (jnp.exp(jnp.sum(lq1_ref[...] * lk1_ref[...], axis=-1, keepdims=True))
           - jnp.exp(jnp.sum(lq2_ref[...] * lk2_ref[...], axis=-1, keepdims=True)) + lam_init)
    on = acc_ref[...] / l_ref[...]
    o = (on[:, :tq] - lam * on[:, tq:]).T
    o_ref[...] = (_rms_rows(o, subln_ref[...], 1e-5) * out_scale).astype(o_ref.dtype)


def _diff_attn(u3, lq1, lk1, lq2, lk2, cst, subln, layer):
    bn, s_len, _ = u3.shape
    tq = ATT_TQ
    nq = s_len // tq
    wv = DIFF_VAL_DIM
    vec = lambda width: pl.BlockSpec((None, 1, width), lambda b, h, i: (layer, 0, 0))
    return pl.pallas_call(
        _diff_attn_kernel,
        grid=(bn, DIFF_HEADS, nq),
        in_specs=[
            vec(DIFF_HEAD_DIM), vec(DIFF_HEAD_DIM), vec(DIFF_HEAD_DIM), vec(DIFF_HEAD_DIM),
            vec(128), vec(wv),
            pl.BlockSpec((None, tq, wv), lambda b, h, i: (b, i, COL_DQ // wv + h)),
            pl.BlockSpec((None, s_len, wv), lambda b, h, i: (b, 0, COL_DK // wv + h)),
            pl.BlockSpec((None, s_len, wv), lambda b, h, i: (b, 0, COL_DV // wv + h)),
        ],
        out_specs=pl.BlockSpec((None, tq, wv), lambda b, h, i: (b, i, h)),
        out_shape=jax.ShapeDtypeStruct((bn, s_len, DIFF_HEADS * wv), BF16),
        scratch_shapes=[
            pltpu.VMEM((s_len, wv), BF16),
            pltpu.VMEM((s_len // ATT_TK, wv, ATT_TK), BF16),
            pltpu.VMEM((1, 2 * tq), F32),
            pltpu.VMEM((1, 2 * tq), F32),
            pltpu.VMEM((wv, 2 * tq), F32),
        ],
        compiler_params=_params("parallel", "parallel", "arbitrary"),
        name="diff_attn",
    )(lq1, lk1, lq2, lk2, cst, subln, u3, u3, u3)


def _retention_kernel(lg_ref, cos_ref, sin_ref, q_ref, k_ref, v_ref, g_ref, o_ref, state_ref):
    c = RET_CHUNK
    dk, dv = RET_KEY_DIM, RET_VAL_DIM
    n_chunks = q_ref.shape[0] // c
    lg = lg_ref[...]
    row = lax.broadcasted_iota(jnp.int32, (c, c), 0).astype(F32)
    col = lax.broadcasted_iota(jnp.int32, (c, c), 1).astype(F32)
    rel = row - col
    decay = jnp.where(rel >= 0, jnp.exp(jnp.maximum(rel, 0.0) * lg[:, :c]), 0.0)
    w_k = jnp.exp((c - 1.0 - row) * lg[:, :c])
    row_v = lax.broadcasted_iota(jnp.int32, (c, dv), 0).astype(F32)
    w_q = jnp.exp((row_v + 1.0) * lg)
    chunk_decay = jnp.exp(float(c) * lg)
    even = (lax.broadcasted_iota(jnp.int32, (c, dk), 1) % 2) == 0

    def rotary(t, cos, sin):
        rot = jnp.where(even, -pltpu.roll(t, dk - 1, axis=1), pltpu.roll(t, 1, axis=1))
        return t * cos + rot * sin

    state_ref[...] = jnp.zeros(state_ref.shape, F32)

    def chunk(n, carry):
        rows = pl.ds(pl.multiple_of(n * c, c), c)
        cos, sin = cos_ref[rows, :], sin_ref[rows, :]
        qr = rotary(q_ref[rows, :], cos, sin)
        kr = rotary(k_ref[rows, :], cos, sin) * (dk ** -0.5)
        qb = qr.astype(BF16)
        vb = v_ref[rows, :].astype(BF16)
        s = lax.dot_general(qb, kr.astype(BF16), (((1,), (1,)), ((), ())), preferred_element_type=F32) * decay
        state = state_ref[...]
        o = (jnp.dot(s.astype(BF16), vb, preferred_element_type=F32)
             + jnp.dot(qb, state.astype(BF16), preferred_element_type=F32) * w_q)
        kv = jnp.dot((kr * w_k).T.astype(BF16), vb, preferred_element_type=F32)
        state_ref[...] = chunk_decay * state + kv
        mu = jnp.mean(o, axis=-1, keepdims=True)
        var = jnp.mean(jnp.square(o - mu), axis=-1, keepdims=True)
        on = (o - mu) * lax.rsqrt(var + 1e-5)
        gate = g_ref[rows, :]
        o_ref[rows, :] = (gate * jax.nn.sigmoid(gate) * on).astype(o_ref.dtype)
        return carry

    lax.fori_loop(0, n_chunks, chunk, 0)


def _retention(u3, lg, cos, sin):
    bn, s_len, _ = u3.shape
    dk, dv = RET_KEY_DIM, RET_VAL_DIM
    return pl.pallas_call(
        _retention_kernel,
        grid=(bn, RET_HEADS),
        in_specs=[
            pl.BlockSpec((None, 1, dv), lambda b, h: (h, 0, 0)),
            pl.BlockSpec((s_len, dk), lambda b, h: (0, 0)),
            pl.BlockSpec((s_len, dk), lambda b, h: (0, 0)),
            pl.BlockSpec((None, s_len, dk), lambda b, h: (b, 0, COL_RQ // dk + h)),
            pl.BlockSpec((None, s_len, dk), lambda b, h: (b, 0, COL_RK // dk + h)),
            pl.BlockSpec((None, s_len, dv), lambda b, h: (b, 0, COL_RV // dv + h)),
            pl.BlockSpec((None, s_len, dv), lambda b, h: (b, 0, COL_RG // dv + h)),
        ],
        out_specs=pl.BlockSpec((None, s_len, dv), lambda b, h: (b, 0, h)),
        out_shape=jax.ShapeDtypeStruct((bn, s_len, RET_HEADS * dv), BF16),
        scratch_shapes=[pltpu.VMEM((dk, dv), F32)],
        compiler_params=_params("parallel", "parallel"),
        name="retention",
    )(lg, cos, sin, u3, u3, u3, u3)


def _softplus(z):
    return jnp.maximum(z, 0.0) + jnp.log1p(jnp.exp(-jnp.abs(z)))


def _gelu_tanh(y):
    return 0.5 * y * (1.0 + jnp.tanh(math.sqrt(2.0 / math.pi) * (y + 0.044715 * (y * y * y))))


def _rglru_kernel(cw_ref, cb_ref, wa_ref, ba_ref, wx_ref, bx_ref, lam_ref, x_ref, y_ref, o_ref,
                  xpad_ref, h_ref):
    ts = x_ref.shape[0]
    halo = SUBLANES
    groups = ts // SUBLANES

    @pl.when(pl.program_id(1) == 0)
    def _():
        xpad_ref[0:halo, :] = jnp.zeros((halo, xpad_ref.shape[1]), F32)
        h_ref[...] = jnp.zeros(h_ref.shape, F32)

    @pl.when(pl.program_id(1) > 0)
    def _():
        xpad_ref[0:halo, :] = xpad_ref[ts:ts + halo, :]

    xpad_ref[halo:halo + ts, :] = x_ref[...]
    sub = lax.broadcasted_iota(jnp.int32, (groups, SUBLANES, LRU_BLOCK_W), 1)

    for n in range(LRU_BLOCKS):
        cols = slice(n * LRU_BLOCK_W, (n + 1) * LRU_BLOCK_W)
        xc = cb_ref[:, cols]
        for j in range(CONV_WIDTH):
            off = halo + j - (CONV_WIDTH - 1)
            xc = xc + cw_ref[j:j + 1, cols] * xpad_ref[off:off + ts, cols]
        xb = xc.astype(BF16)
        r = jax.nn.sigmoid(jnp.dot(xb, wa_ref[n], preferred_element_type=F32) + ba_ref[:, cols])
        gi = jax.nn.sigmoid(jnp.dot(xb, wx_ref[n], preferred_element_type=F32) + bx_ref[:, cols])
        log_a = (-LRU_C * _softplus(-lam_ref[:, cols])) * r
        a = jnp.exp(log_a)
        b = jnp.sqrt(1.0 - jnp.exp(2.0 * log_a)) * (gi * xc)

        a3 = a.reshape(groups, SUBLANES, LRU_BLOCK_W)
        b3 = b.reshape(groups, SUBLANES, LRU_BLOCK_W)
        d = 1
        while d < SUBLANES:
            keep = sub >= d
            b3 = jnp.where(keep, a3 * pltpu.roll(b3, d, axis=1) + b3, b3)
            a3 = jnp.where(keep, a3 * pltpu.roll(a3, d, axis=1), a3)
            d *= 2
        h = h_ref[:, cols]
        y = y_ref[:, cols]
        for g in range(groups):
            hg = b3[g] + a3[g] * h
            h = hg[SUBLANES - 1:SUBLANES, :]
            rows = slice(g * SUBLANES, (g + 1) * SUBLANES)
            o_ref[rows, cols] = (hg * _gelu_tanh(y[rows, :])).astype(o_ref.dtype)
        h_ref[:, cols] = h


def _rglru(u3, conv_w, conv_b, wa, ba, wx, bx, lam, layer):
    bn, s_len, _ = u3.shape
    w = LRU_BLOCKS * LRU_BLOCK_W
    ts = min(LRU_TS, s_len)
    vec = lambda: pl.BlockSpec((None, 1, w), lambda b, t: (layer, 0, 0))
    blk = lambda: pl.BlockSpec((None, LRU_BLOCKS, LRU_BLOCK_W, LRU_BLOCK_W), lambda b, t: (layer, 0, 0, 0))
    return pl.pallas_call(
        _rglru_kernel,
        grid=(bn, s_len // ts),
        in_specs=[
            pl.BlockSpec((None, CONV_WIDTH, w), lambda b, t: (layer, 0, 0)),
            vec(), blk(), vec(), blk(), vec(), vec(),
            pl.BlockSpec((None, ts, w), lambda b, t: (b, t, COL_LX // w)),
            pl.BlockSpec((None, ts, w), lambda b, t: (b, t, COL_LY // w)),
        ],
        out_specs=pl.BlockSpec((None, ts, w), lambda b, t: (b, t, 0)),
        out_shape=jax.ShapeDtypeStruct((bn, s_len, w), BF16),
        scratch_shapes=[pltpu.VMEM((ts + SUBLANES, w), F32), pltpu.VMEM((1, w), F32)],
        compiler_params=_params("parallel", "arbitrary"),
        name="rglru",
    )(conv_w, conv_b, wa, ba, wx, bx, lam, u3, u3)


def _merge_kernel(od_ref, or_ref, ol_ref, g0_ref, g1_ref, g2_ref, x_ref, wb_ref, wo_ref, o_ref):
    merged = (jax.nn.sigmoid(g0_ref[...]) * jnp.dot(od_ref[...], wb_ref[0], preferred_element_type=F32)
              + jax.nn.sigmoid(g1_ref[...]) * jnp.dot(or_ref[...], wb_ref[1], preferred_element_type=F32)
              + jax.nn.sigmoid(g2_ref[...]) * jnp.dot(ol_ref[...], wb_ref[2], preferred_element_type=F32))
    o_ref[...] = x_ref[...] + jnp.dot(merged.astype(BF16), wo_ref[...], preferred_element_type=F32)


def _merge(od, o_r, o_l, u2, x2, w_branch, w_out, layer, tm=256):
    m, d = x2.shape
    tm = min(tm, m)
    row = lambda: pl.BlockSpec((tm, d), lambda i: (i, 0))
    gate = lambda k: pl.BlockSpec((tm, d), lambda i: (i, COL_GATES // d + k))
    return pl.pallas_call(
        _merge_kernel,
        grid=(m // tm,),
        in_specs=[
            row(), row(), row(), gate(0), gate(1), gate(2), row(),
            pl.BlockSpec((None, 3, d, d), lambda i: (layer, 0, 0, 0)),
            pl.BlockSpec((None, d, d), lambda i: (layer, 0, 0)),
        ],
        out_specs=row(),
        out_shape=jax.ShapeDtypeStruct((m, d), F32),
        compiler_params=_params("parallel"),
        name="merge",
    )(od, o_r, o_l, u2, u2, u2, x2, w_branch, w_out)


def _xattn_kernel(x_ref, g_ref, wq_ref, kv_ref, wo_ref, o_ref):
    x = x_ref[...]
    h = _rms_rows(x, g_ref[...], 1e-6).astype(BF16)
    q = jnp.dot(h, wq_ref[...], preferred_element_type=F32).astype(BF16)
    heads = []
    for hd in range(XA_HEADS):
        kc = slice(hd * XA_HEAD_DIM, (hd + 1) * XA_HEAD_DIM)
        vc = slice(XA_WIDTH + hd * XA_HEAD_DIM, XA_WIDTH + (hd + 1) * XA_HEAD_DIM)
        s = lax.dot_general(q[:, kc], kv_ref[:, kc], (((1,), (1,)), ((), ())),
                            preferred_element_type=F32) * (XA_HEAD_DIM ** -0.5)
        e = jnp.exp(s - jnp.max(s, axis=-1, keepdims=True))
        pv = jnp.dot(e.astype(BF16), kv_ref[:, vc], preferred_element_type=F32)
        heads.append((pv / jnp.sum(e, axis=-1, keepdims=True)).astype(BF16))
    o = jnp.concatenate(heads, axis=-1)
    o_ref[...] = x + jnp.dot(o, wo_ref[...], preferred_element_type=F32)


def _xattn(x3, g, wq, kv3, wo, layer, tm=512):
    bn, s_len, d = x3.shape
    tm = min(tm, s_len)
    m_len = kv3.shape[1]
    return pl.pallas_call(
        _xattn_kernel,
        grid=(bn, s_len // tm),
        in_specs=[
            pl.BlockSpec((None, tm, d), lambda b, i: (b, i, 0)),
            pl.BlockSpec((None, 1, d), lambda b, i: (layer, 0, 0)),
            pl.BlockSpec((None, d, XA_WIDTH), lambda b, i: (layer, 0, 0)),
            pl.BlockSpec((None, m_len, 2 * XA_WIDTH), lambda b, i: (b, 0, 0)),
            pl.BlockSpec((None, XA_WIDTH, d), lambda b, i: (layer, 0, 0)),
        ],
        out_specs=pl.BlockSpec((None, tm, d), lambda b, i: (b, i, 0)),
        out_shape=jax.ShapeDtypeStruct((bn, s_len, d), F32),
        compiler_params=_params("parallel", "parallel"),
        name="xattn",
    )(x3, g, wq, kv3, wo)


def _mlp_kernel(x_ref, g_ref, w1_ref, w2_ref, o_ref):
    x = x_ref[...]
    h = _rms_rows(x, g_ref[...], 1e-6).astype(BF16)
    d = x.shape[1]
    acc = x
    for c in range(w1_ref.shape[1] // d):
        cols = slice(c * d, (c + 1) * d)
        a = jnp.maximum(jnp.dot(h, w1_ref[:, cols], preferred_element_type=F32), 0.0)
        acc = acc + jnp.dot((a * a).astype(BF16), w2_ref[cols, :], preferred_element_type=F32)
    o_ref[...] = acc


def _mlp(x2, g, w1, w2, layer, tm=512):
    m, d = x2.shape
    tm = min(tm, m)
    dff = w1.shape[2]
    return pl.pallas_call(
        _mlp_kernel,
        grid=(m // tm,),
        in_specs=[
            pl.BlockSpec((tm, d), lambda i: (i, 0)),
            pl.BlockSpec((None, 1, d), lambda i: (layer, 0, 0)),
            pl.BlockSpec((None, d, dff), lambda i: (layer, 0, 0)),
            pl.BlockSpec((None, dff, d), lambda i: (layer, 0, 0)),
        ],
        out_specs=pl.BlockSpec((tm, d), lambda i: (i, 0)),
        out_shape=jax.ShapeDtypeStruct((m, d), F32),
        compiler_params=_params("parallel"),
        name="mlp",
    )(x2, g, w1, w2)


def _final_norm_kernel(x_ref, g_ref, o_ref):
    o_ref[...] = _rms_rows(x_ref[...], g_ref[...], 1e-6)


def _final_norm(x2, g, tm=512):
    m, d = x2.shape
    tm = min(tm, m)
    return pl.pallas_call(
        _final_norm_kernel,
        grid=(m // tm,),
        in_specs=[pl.BlockSpec((tm, d), lambda i: (i, 0)), pl.BlockSpec((1, d), lambda i: (0, 0))],
        out_specs=pl.BlockSpec((tm, d), lambda i: (i, 0)),
        out_shape=jax.ShapeDtypeStruct((m, d), F32),
        compiler_params=_params("parallel"),
        name="final_norm",
    )(x2, g)


def kernel(x, mem, norm_mix, w_in, diff_lq1, diff_lk1, diff_lq2, diff_lk2, diff_subln, lru_conv_w, lru_conv_b, lru_wa, lru_ba, lru_wx, lru_bx, lru_lambda, w_branch, w_out, norm_xattn, norm_mem, xa_wq, xa_wkv, xa_wo, norm_mlp, mlp_w1, mlp_w2, norm_final):
    bn, s_len, d = x.shape
    m_len = mem.shape[1]
    depth = norm_mix.shape[0]
    rows = bn * s_len

    w_in_b, w_branch_b, w_out_b = w_in.astype(BF16), w_branch.astype(BF16), w_out.astype(BF16)
    wq_b, wkv_b, wo_b = xa_wq.astype(BF16), xa_wkv.astype(BF16), xa_wo.astype(BF16)
    w1_b, w2_b = mlp_w1.astype(BF16), mlp_w2.astype(BF16)
    wa_b, wx_b = lru_wa.astype(BF16), lru_wx.astype(BF16)
    row3 = lambda p: p.reshape(depth, 1, p.shape[-1])
    conv_w = lru_conv_w.reshape(depth, CONV_WIDTH, d)

    pos = jnp.arange(s_len, dtype=F32)
    angle = jnp.repeat(1.0 / (10000.0 ** jnp.linspace(0.0, 1.0, RET_KEY_DIM // 2, dtype=F32)), 2)
    phase = pos[:, None] * angle[None, :]
    cos, sin = jnp.cos(phase), jnp.sin(phase)
    log_g = jnp.log(1.0 - jnp.exp2(-5.0 - jnp.arange(RET_HEADS, dtype=F32)))
    lg = jnp.broadcast_to(log_g[:, None, None], (RET_HEADS, 1, RET_VAL_DIM))
    lam_inits = [0.8 - 0.6 * math.exp(-0.3 * l) for l in range(depth)]
    cst = jnp.zeros((depth, 1, 128), F32)
    cst = cst.at[:, 0, 0].set(jnp.asarray(lam_inits, F32))
    cst = cst.at[:, 0, 1].set(jnp.asarray([1.0 - v for v in lam_inits], F32))

    x2 = x.reshape(rows, d)
    mem2 = mem.reshape(bn * m_len, d)
    for l in range(depth):
        u2 = _norm_matmul(x2, row3(norm_mix), w_in_b, l, IN_COLS, F32, tm=1024, tn=1024)
        u3 = u2.reshape(bn, s_len, IN_COLS)
        od = _diff_attn(u3, row3(diff_lq1), row3(diff_lk1), row3(diff_lq2), row3(diff_lk2), cst,
                        row3(diff_subln), l)
        o_r = _retention(u3, lg, cos, sin)
        o_l = _rglru(u3, conv_w, row3(lru_conv_b), wa_b, row3(lru_ba), wx_b, row3(lru_bx),
                     row3(lru_lambda), l)
        x2 = _merge(od.reshape(rows, d), o_r.reshape(rows, d), o_l.reshape(rows, d), u2, x2,
                    w_branch_b, w_out_b, l)
        kv = _norm_matmul(mem2, row3(norm_mem), wkv_b, l, 2 * XA_WIDTH, BF16, tm=512, tn=2 * XA_WIDTH)
        x2 = _xattn(x2.reshape(bn, s_len, d), row3(norm_xattn), wq_b, kv.reshape(bn, m_len, 2 * XA_WIDTH),
                    wo_b, l).reshape(rows, d)
        x2 = _mlp(x2, row3(norm_mlp), w1_b, w2_b, l)
    return _final_norm(x2, norm_final.reshape(1, d)).reshape(bn, s_len, d)
```

```python
import functools
import math

import jax
import jax.numpy as jnp
from jax import lax
from jax.experimental import pallas as pl
from jax.experimental.pallas import tpu as pltpu

F32 = jnp.float32
BF16 = jnp.bfloat16

D_MODEL = 1024
DIFF_HEADS = 8
DIFF_HEAD_DIM = 64
DIFF_VAL_DIM = 2 * DIFF_HEAD_DIM
RET_HEADS = 4
RET_KEY_DIM = 128
RET_VAL_DIM = 256
RET_CHUNK = 128
LRU_BLOCKS = 8
LRU_BLOCK_W = 128
CONV_WIDTH = 4
LRU_C = 8.0
XA_HEADS = 4
XA_HEAD_DIM = 128
XA_WIDTH = XA_HEADS * XA_HEAD_DIM
D_FF = 4 * D_MODEL

PROJ_TN = 1024
PROJ_TILE_KINDS = ("b", "b", "b", "f", "b", "f", "f", "f", "f", "f", "f")
COLB_DQ, COLB_DK, COLB_DV, COLB_RV = 0, 1024, 2048, 3072
COLF_RQ, COLF_RK, COLF_RG, COLF_LX, COLF_LY, COLF_GATES = 0, 512, 1024, 2048, 3072, 4096

VMEM_LIMIT_BYTES = 56 * 1024 * 1024

ATT_TQ = 256
ATT_TK = 256
ATT_HEADS_PER_STEP = 4
RET_CHUNKS_PER_TRIP = 4
LRU_TS = 256
SUBLANES = 8


def _params(*sem):
    return pltpu.CompilerParams(dimension_semantics=sem, vmem_limit_bytes=VMEM_LIMIT_BYTES)


def _rms_rows(xf, g, eps):
    return xf * lax.rsqrt(jnp.mean(xf * xf, axis=-1, keepdims=True) + eps) * g


def _norm_matmul_kernel(x_ref, g_ref, w_ref, o_ref, h_ref):
    @pl.when(pl.program_id(1) == 0)
    def _():
        h_ref[...] = _rms_rows(x_ref[...], g_ref[...], 1e-6).astype(BF16)

    o_ref[...] = jnp.dot(h_ref[...], w_ref[...], preferred_element_type=F32).astype(o_ref.dtype)


def _norm_matmul(x2, g, w, layer, n_cols, out_dtype, tm, tn):
    m, d = x2.shape
    tm = min(tm, m)
    return pl.pallas_call(
        _norm_matmul_kernel,
        grid=(m // tm, n_cols // tn),
        in_specs=[
            pl.BlockSpec((tm, d), lambda i, j: (i, 0)),
            pl.BlockSpec((None, 1, d), lambda i, j: (layer, 0, 0)),
            pl.BlockSpec((None, d, tn), lambda i, j: (layer, 0, j)),
        ],
        out_specs=pl.BlockSpec((tm, tn), lambda i, j: (i, j)),
        out_shape=jax.ShapeDtypeStruct((m, n_cols), out_dtype),
        scratch_shapes=[pltpu.VMEM((tm, d), BF16)],
        compiler_params=_params("parallel", "arbitrary"),
        name="norm_matmul",
    )(x2, g, w)


def _held_index(kind):
    table, n = [], 0
    for t in PROJ_TILE_KINDS:
        if t == kind:
            n += 1
        table.append(max(n - 1, 0))
    steps = [j for j in range(1, len(table)) if table[j] != table[j - 1]]
    return lambda j: sum((j >= s).astype(jnp.int32) for s in steps)


def _proj_in_kernel(x_ref, g_ref, w_ref, ob_ref, of_ref, h_ref):
    j = pl.program_id(1)

    @pl.when(j == 0)
    def _():
        h_ref[...] = _rms_rows(x_ref[...], g_ref[...], 1e-6).astype(BF16)

    to_bf16 = functools.reduce(jnp.logical_or, [j == t for t, k in enumerate(PROJ_TILE_KINDS) if k == "b"])

    @pl.when(to_bf16)
    def _():
        ob_ref[...] = jnp.dot(h_ref[...], w_ref[...], preferred_element_type=F32).astype(BF16)

    @pl.when(jnp.logical_not(to_bf16))
    def _():
        of_ref[...] = jnp.dot(h_ref[...], w_ref[...], preferred_element_type=F32)


def _proj_in(x2, g, w, layer, tm=1024):
    m, d = x2.shape
    tm = min(tm, m)
    tn = PROJ_TN
    idx_b, idx_f = _held_index("b"), _held_index("f")
    return pl.pallas_call(
        _proj_in_kernel,
        grid=(m // tm, len(PROJ_TILE_KINDS)),
        in_specs=[
            pl.BlockSpec((tm, d), lambda i, j: (i, 0)),
            pl.BlockSpec((None, 1, d), lambda i, j: (layer, 0, 0)),
            pl.BlockSpec((None, d, tn), lambda i, j: (layer, 0, j)),
        ],
        out_specs=[
            pl.BlockSpec((tm, tn), lambda i, j: (i, idx_b(j))),
            pl.BlockSpec((tm, tn), lambda i, j: (i, idx_f(j))),
        ],
        out_shape=[
            jax.ShapeDtypeStruct((m, tn * PROJ_TILE_KINDS.count("b")), BF16),
            jax.ShapeDtypeStruct((m, tn * PROJ_TILE_KINDS.count("f")), F32),
        ],
        scratch_shapes=[pltpu.VMEM((tm, d), BF16)],
        compiler_params=_params("parallel", "arbitrary"),
        name="proj_in",
    )(x2, g, w)


def _diff_attn_kernel(lq1_ref, lk1_ref, lq2_ref, lk2_ref, cst_ref, subln_ref, q_ref, k_ref, v_ref,
                      o_ref, qq_ref, vt_ref, s_ref, m_ref, l_ref, acc_ref):
    i = pl.program_id(2)
    tq, tk, wv = ATT_TQ, ATT_TK, DIFF_VAL_DIM
    n_kv = k_ref.shape[0] // tk
    heads = range(ATT_HEADS_PER_STEP)

    @pl.when(i == 0)
    def _():
        for hd in heads:
            for t in range(n_kv):
                v_t = v_ref[t * tk:(t + 1) * tk, hd * wv:(hd + 1) * wv]
                vt_ref[hd, t] = v_t.astype(F32).T.astype(BF16)

    lane = lax.broadcasted_iota(jnp.int32, (tq, wv), 1)
    for hd in heads:
        q = q_ref[:, hd * wv:(hd + 1) * wv].astype(F32) * (DIFF_HEAD_DIM ** -0.5)
        qq_ref[hd, 0:tq, :] = jnp.where(lane < DIFF_HEAD_DIM, q, 0.0).astype(BF16)
        qq_ref[hd, tq:2 * tq, :] = jnp.where(lane >= DIFF_HEAD_DIM, q, 0.0).astype(BF16)

    m_ref[...] = jnp.full(m_ref.shape, -jnp.inf, F32)
    l_ref[...] = jnp.zeros(l_ref.shape, F32)
    acc_ref[...] = jnp.zeros(acc_ref.shape, F32)

    def scores(j, hd, masked):
        k_t = k_ref[pl.ds(pl.multiple_of(j * tk, tk), tk), hd * wv:(hd + 1) * wv]
        s = lax.dot_general(k_t, qq_ref[hd], (((1,), (1,)), ((), ())), preferred_element_type=F32)
        if masked:
            key = lax.broadcasted_iota(jnp.int32, (tk, tq), 0)
            qry = lax.broadcasted_iota(jnp.int32, (tk, tq), 1)
            ok = key <= qry
            s = jnp.where(jnp.concatenate([ok, ok], axis=1), s, -jnp.inf)
        s_ref[hd] = s

    def accumulate(j, hd):
        s = s_ref[hd]
        m_old = m_ref[hd]
        m_new = jnp.maximum(m_old, jnp.max(s, axis=0, keepdims=True))
        alpha = jnp.exp(m_old - m_new)
        e = jnp.exp(s - m_new)
        l_ref[hd] = alpha * l_ref[hd] + jnp.sum(e, axis=0, keepdims=True)
        acc_ref[hd] = alpha * acc_ref[hd] + jnp.dot(vt_ref[hd, j], e.astype(BF16), preferred_element_type=F32)
        m_ref[hd] = m_new

    def tiles(j, masked):
        for hd in heads:
            scores(j, hd, masked)
        for hd in heads:
            accumulate(j, hd)

    def body(j, carry):
        tiles(j, False)
        return carry

    lax.fori_loop(0, i, body, 0)
    tiles(i, True)

    lam_init = cst_ref[0:1, 0:1]
    out_scale = cst_ref[0:1, 1:2]
    lam = (jnp.exp(jnp.sum(lq1_ref[...] * lk1_ref[...], axis=-1, keepdims=True))
           - jnp.exp(jnp.sum(lq2_ref[...] * lk2_ref[...], axis=-1, keepdims=True)) + lam_init)
    for hd in heads:
        on = acc_ref[hd] / l_ref[hd]
        o = (on[:, :tq] - lam * on[:, tq:]).T
        o_ref[:, hd * wv:(hd + 1) * wv] = (_rms_rows(o, subln_ref[...], 1e-5) * out_scale).astype(o_ref.dtype)


def _diff_attn(ub3, lq1, lk1, lq2, lk2, cst, subln, layer):
    bn, s_len, _ = ub3.shape
    tq, hp = ATT_TQ, ATT_HEADS_PER_STEP
    assert s_len % tq == 0 and DIFF_HEADS % hp == 0
    nq = s_len // tq
    wv = DIFF_VAL_DIM
    wb = hp * wv
    vec = lambda width: pl.BlockSpec((None, 1, width), lambda b, h, i: (layer, 0, 0))
    return pl.pallas_call(
        _diff_attn_kernel,
        grid=(bn, DIFF_HEADS // hp, nq),
        in_specs=[
            vec(DIFF_HEAD_DIM), vec(DIFF_HEAD_DIM), vec(DIFF_HEAD_DIM), vec(DIFF_HEAD_DIM),
            vec(128), vec(wv),
            pl.BlockSpec((None, tq, wb), lambda b, h, i: (b, i, COLB_DQ // wb + h)),
            pl.BlockSpec((None, s_len, wb), lambda b, h, i: (b, 0, COLB_DK // wb + h)),
            pl.BlockSpec((None, s_len, wb), lambda b, h, i: (b, 0, COLB_DV // wb + h)),
        ],
        out_specs=pl.BlockSpec((None, tq, wb), lambda b, h, i: (b, i, h)),
        out_shape=jax.ShapeDtypeStruct((bn, s_len, DIFF_HEADS * wv), BF16),
        scratch_shapes=[
            pltpu.VMEM((hp, 2 * tq, wv), BF16),
            pltpu.VMEM((hp, s_len // ATT_TK, wv, ATT_TK), BF16),
            pltpu.VMEM((hp, ATT_TK, 2 * tq), F32),
            pltpu.VMEM((hp, 1, 2 * tq), F32),
            pltpu.VMEM((hp, 1, 2 * tq), F32),
            pltpu.VMEM((hp, wv, 2 * tq), F32),
        ],
        compiler_params=_params("parallel", "parallel", "arbitrary"),
        name="diff_attn",
    )(lq1, lk1, lq2, lk2, cst, subln, ub3, ub3, ub3)


def _retention_kernel(lg_ref, cos_ref, sin_ref, q_ref, k_ref, v_ref, g_ref, o_ref, state_ref):
    c = RET_CHUNK
    dk, dv = RET_KEY_DIM, RET_VAL_DIM
    n_chunks = q_ref.shape[0] // c
    lg = lg_ref[...]
    row = lax.broadcasted_iota(jnp.int32, (c, c), 0).astype(F32)
    col = lax.broadcasted_iota(jnp.int32, (c, c), 1).astype(F32)
    rel = row - col
    decay = jnp.where(rel >= 0, jnp.exp(jnp.maximum(rel, 0.0) * lg[:, :c]), 0.0)
    w_k = jnp.exp((c - 1.0 - row) * lg[:, :c])
    row_v = lax.broadcasted_iota(jnp.int32, (c, dv), 0).astype(F32)
    w_q = jnp.exp((row_v + 1.0) * lg)
    chunk_decay = jnp.exp(float(c) * lg)

    def rotary(t, cos, sin):
        return t * cos + pltpu.roll(t, dk // 2, axis=1) * sin

    state_ref[...] = jnp.zeros(state_ref.shape, F32)

    def chunk(n):
        rows = pl.ds(pl.multiple_of(n * c, c), c)
        cos, sin = cos_ref[rows, :], sin_ref[rows, :]
        qr = rotary(q_ref[rows, :], cos, sin)
        kr = rotary(k_ref[rows, :], cos, sin) * (dk ** -0.5)
        qb = qr.astype(BF16)
        vb = v_ref[rows, :]
        s = lax.dot_general(qb, kr.astype(BF16), (((1,), (1,)), ((), ())), preferred_element_type=F32) * decay
        state = state_ref[...]
        o = (jnp.dot(s.astype(BF16), vb, preferred_element_type=F32)
             + jnp.dot(qb, state.astype(BF16), preferred_element_type=F32) * w_q)
        kv = jnp.dot((kr * w_k).T.astype(BF16), vb, preferred_element_type=F32)
        state_ref[...] = chunk_decay * state + kv
        mu = jnp.mean(o, axis=-1, keepdims=True)
        var = jnp.mean(jnp.square(o - mu), axis=-1, keepdims=True)
        on = (o - mu) * lax.rsqrt(var + 1e-5)
        gate = g_ref[rows, :]
        o_ref[rows, :] = (gate * jax.nn.sigmoid(gate) * on).astype(o_ref.dtype)

    def body(nn, carry):
        for t in range(RET_CHUNKS_PER_TRIP):
            chunk(nn * RET_CHUNKS_PER_TRIP + t)
        return carry

    lax.fori_loop(0, n_chunks // RET_CHUNKS_PER_TRIP, body, 0)


def _retention(ub3, uf3, lg, cos, sin):
    bn, s_len, _ = uf3.shape
    dk, dv = RET_KEY_DIM, RET_VAL_DIM
    assert s_len % (RET_CHUNK * RET_CHUNKS_PER_TRIP) == 0
    return pl.pallas_call(
        _retention_kernel,
        grid=(bn, RET_HEADS),
        in_specs=[
            pl.BlockSpec((None, 1, dv), lambda b, h: (h, 0, 0)),
            pl.BlockSpec((s_len, dk), lambda b, h: (0, 0)),
            pl.BlockSpec((s_len, dk), lambda b, h: (0, 0)),
            pl.BlockSpec((None, s_len, dk), lambda b, h: (b, 0, COLF_RQ // dk + h)),
            pl.BlockSpec((None, s_len, dk), lambda b, h: (b, 0, COLF_RK // dk + h)),
            pl.BlockSpec((None, s_len, dv), lambda b, h: (b, 0, COLB_RV // dv + h)),
            pl.BlockSpec((None, s_len, dv), lambda b, h: (b, 0, COLF_RG // dv + h)),
        ],
        out_specs=pl.BlockSpec((None, s_len, dv), lambda b, h: (b, 0, h)),
        out_shape=jax.ShapeDtypeStruct((bn, s_len, RET_HEADS * dv), BF16),
        scratch_shapes=[pltpu.VMEM((dk, dv), F32)],
        compiler_params=_params("parallel", "parallel"),
        name="retention",
    )(lg, cos, sin, uf3, uf3, ub3, uf3)


def _softplus(z):
    return jnp.maximum(z, 0.0) + jnp.log1p(jnp.exp(-jnp.abs(z)))


def _gelu_tanh(y):
    return 0.5 * y * (1.0 + jnp.tanh(math.sqrt(2.0 / math.pi) * (y + 0.044715 * (y * y * y))))


def _rglru_kernel(cw_ref, cb_ref, wa_ref, ba_ref, wx_ref, bx_ref, lam_ref, x_ref, y_ref, o_ref,
                  xpad_ref, h_ref):
    ts = x_ref.shape[0]
    halo = SUBLANES
    groups = ts // SUBLANES

    @pl.when(pl.program_id(1) == 0)
    def _():
        xpad_ref[0:halo, :] = jnp.zeros((halo, xpad_ref.shape[1]), F32)
        h_ref[...] = jnp.zeros(h_ref.shape, F32)

    @pl.when(pl.program_id(1) > 0)
    def _():
        xpad_ref[0:halo, :] = xpad_ref[ts:ts + halo, :]

    xpad_ref[halo:halo + ts, :] = x_ref[...]
    sub = lax.broadcasted_iota(jnp.int32, (groups, SUBLANES, LRU_BLOCK_W), 1)

    for n in range(LRU_BLOCKS):
        cols = slice(n * LRU_BLOCK_W, (n + 1) * LRU_BLOCK_W)
        xc = cb_ref[:, cols]
        for j in range(CONV_WIDTH):
            off = halo + j - (CONV_WIDTH - 1)
            xc = xc + cw_ref[j:j + 1, cols] * xpad_ref[off:off + ts, cols]
        xb = xc.astype(BF16)
        r = jax.nn.sigmoid(jnp.dot(xb, wa_ref[n], preferred_element_type=F32) + ba_ref[:, cols])
        gi = jax.nn.sigmoid(jnp.dot(xb, wx_ref[n], preferred_element_type=F32) + bx_ref[:, cols])
        log_a = (-LRU_C * _softplus(-lam_ref[:, cols])) * r
        a = jnp.exp(log_a)
        b = jnp.sqrt(1.0 - a * a) * (gi * xc)

        a3 = a.reshape(groups, SUBLANES, LRU_BLOCK_W)
        b3 = b.reshape(groups, SUBLANES, LRU_BLOCK_W)
        d = 1
        while d < SUBLANES:
            keep = sub >= d
            b3 = jnp.where(keep, a3 * pltpu.roll(b3, d, axis=1) + b3, b3)
            a3 = jnp.where(keep, a3 * pltpu.roll(a3, d, axis=1), a3)
            d *= 2
        h = h_ref[:, cols]
        y = y_ref[:, cols]
        for g in range(groups):
            hg = b3[g] + a3[g] * h
            h = hg[SUBLANES - 1:SUBLANES, :]
            rows = slice(g * SUBLANES, (g + 1) * SUBLANES)
            o_ref[rows, cols] = (hg * _gelu_tanh(y[rows, :])).astype(o_ref.dtype)
        h_ref[:, cols] = h


def _rglru(u3, conv_w, conv_b, wa, ba, wx, bx, lam, layer):
    bn, s_len, _ = u3.shape
    w = LRU_BLOCKS * LRU_BLOCK_W
    ts = min(LRU_TS, s_len)
    vec = lambda: pl.BlockSpec((None, 1, w), lambda b, t: (layer, 0, 0))
    blk = lambda: pl.BlockSpec((None, LRU_BLOCKS, LRU_BLOCK_W, LRU_BLOCK_W), lambda b, t: (layer, 0, 0, 0))
    return pl.pallas_call(
        _rglru_kernel,
        grid=(bn, s_len // ts),
        in_specs=[
            pl.BlockSpec((None, CONV_WIDTH, w), lambda b, t: (layer, 0, 0)),
            vec(), blk(), vec(), blk(), vec(), vec(),
            pl.BlockSpec((None, ts, w), lambda b, t: (b, t, COLF_LX // w)),
            pl.BlockSpec((None, ts, w), lambda b, t: (b, t, COLF_LY // w)),
        ],
        out_specs=pl.BlockSpec((None, ts, w), lambda b, t: (b, t, 0)),
        out_shape=jax.ShapeDtypeStruct((bn, s_len, w), BF16),
        scratch_shapes=[pltpu.VMEM((ts + SUBLANES, w), F32), pltpu.VMEM((1, w), F32)],
        compiler_params=_params("parallel", "arbitrary"),
        name="rglru",
    )(conv_w, conv_b, wa, ba, wx, bx, lam, u3, u3)


def _merge_kernel(od_ref, or_ref, ol_ref, g0_ref, g1_ref, g2_ref, x_ref, wb_ref, wo_ref, o_ref):
    merged = (jax.nn.sigmoid(g0_ref[...]) * jnp.dot(od_ref[...], wb_ref[0], preferred_element_type=F32)
              + jax.nn.sigmoid(g1_ref[...]) * jnp.dot(or_ref[...], wb_ref[1], preferred_element_type=F32)
              + jax.nn.sigmoid(g2_ref[...]) * jnp.dot(ol_ref[...], wb_ref[2], preferred_element_type=F32))
    o_ref[...] = x_ref[...] + jnp.dot(merged.astype(BF16), wo_ref[...], preferred_element_type=F32)


def _merge(od, o_r, o_l, u2, x2, w_branch, w_out, layer, tm=256):
    m, d = x2.shape
    tm = min(tm, m)
    row = lambda: pl.BlockSpec((tm, d), lambda i: (i, 0))
    gate = lambda k: pl.BlockSpec((tm, d), lambda i: (i, COLF_GATES // d + k))
    return pl.pallas_call(
        _merge_kernel,
        grid=(m // tm,),
        in_specs=[
            row(), row(), row(), gate(0), gate(1), gate(2), row(),
            pl.BlockSpec((None, 3, d, d), lambda i: (layer, 0, 0, 0)),
            pl.BlockSpec((None, d, d), lambda i: (layer, 0, 0)),
        ],
        out_specs=row(),
        out_shape=jax.ShapeDtypeStruct((m, d), F32),
        compiler_params=_params("parallel"),
        name="merge",
    )(od, o_r, o_l, u2, u2, u2, x2, w_branch, w_out)


def _xattn_kernel(x_ref, g_ref, wq_ref, kv_ref, wo_ref, o_ref):
    x = x_ref[...]
    h = _rms_rows(x, g_ref[...], 1e-6).astype(BF16)
    q = jnp.dot(h, wq_ref[...], preferred_element_type=F32).astype(BF16)
    heads = []
    for hd in range(XA_HEADS):
        kc = slice(hd * XA_HEAD_DIM, (hd + 1) * XA_HEAD_DIM)
        vc = slice(XA_WIDTH + hd * XA_HEAD_DIM, XA_WIDTH + (hd + 1) * XA_HEAD_DIM)
        s = lax.dot_general(q[:, kc], kv_ref[:, kc], (((1,), (1,)), ((), ())),
                            preferred_element_type=F32) * (XA_HEAD_DIM ** -0.5)
        e = jnp.exp(s - jnp.max(s, axis=-1, keepdims=True))
        pv = jnp.dot(e.astype(BF16), kv_ref[:, vc], preferred_element_type=F32)
        heads.append((pv / jnp.sum(e, axis=-1, keepdims=True)).astype(BF16))
    o = jnp.concatenate(heads, axis=-1)
    o_ref[...] = x + jnp.dot(o, wo_ref[...], preferred_element_type=F32)


def _xattn(x3, g, wq, kv3, wo, layer, tm=512):
    bn, s_len, d = x3.shape
    tm = min(tm, s_len)
    m_len = kv3.shape[1]
    return pl.pallas_call(
        _xattn_kernel,
        grid=(bn, s_len // tm),
        in_specs=[
            pl.BlockSpec((None, tm, d), lambda b, i: (b, i, 0)),
            pl.BlockSpec((None, 1, d), lambda b, i: (layer, 0, 0)),
            pl.BlockSpec((None, d, XA_WIDTH), lambda b, i: (layer, 0, 0)),
            pl.BlockSpec((None, m_len, 2 * XA_WIDTH), lambda b, i: (b, 0, 0)),
            pl.BlockSpec((None, XA_WIDTH, d), lambda b, i: (layer, 0, 0)),
        ],
        out_specs=pl.BlockSpec((None, tm, d), lambda b, i: (b, i, 0)),
        out_shape=jax.ShapeDtypeStruct((bn, s_len, d), F32),
        compiler_params=_params("parallel", "parallel"),
        name="xattn",
    )(x3, g, wq, kv3, wo)


def _mlp_kernel(x_ref, g_ref, w1_ref, w2_ref, o_ref):
    x = x_ref[...]
    h = _rms_rows(x, g_ref[...], 1e-6).astype(BF16)
    d = x.shape[1]
    acc = x
    for c in range(w1_ref.shape[1] // d):
        cols = slice(c * d, (c + 1) * d)
        a = jnp.maximum(jnp.dot(h, w1_ref[:, cols], preferred_element_type=F32), 0.0)
        acc = acc + jnp.dot((a * a).astype(BF16), w2_ref[cols, :], preferred_element_type=F32)
    o_ref[...] = acc


def _mlp(x2, g, w1, w2, layer, tm=512):
    m, d = x2.shape
    tm = min(tm, m)
    dff = w1.shape[2]
    return pl.pallas_call(
        _mlp_kernel,
        grid=(m // tm,),
        in_specs=[
            pl.BlockSpec((tm, d), lambda i: (i, 0)),
            pl.BlockSpec((None, 1, d), lambda i: (layer, 0, 0)),
            pl.BlockSpec((None, d, dff), lambda i: (layer, 0, 0)),
            pl.BlockSpec((None, dff, d), lambda i: (layer, 0, 0)),
        ],
        out_specs=pl.BlockSpec((tm, d), lambda i: (i, 0)),
        out_shape=jax.ShapeDtypeStruct((m, d), F32),
        compiler_params=_params("parallel"),
        name="mlp",
    )(x2, g, w1, w2)


def _final_norm_kernel(x_ref, g_ref, o_ref):
    o_ref[...] = _rms_rows(x_ref[...], g_ref[...], 1e-6)


def _final_norm(x2, g, tm=512):
    m, d = x2.shape
    tm = min(tm, m)
    return pl.pallas_call(
        _final_norm_kernel,
        grid=(m // tm,),
        in_specs=[pl.BlockSpec((tm, d), lambda i: (i, 0)), pl.BlockSpec((1, d), lambda i: (0, 0))],
        out_specs=pl.BlockSpec((tm, d), lambda i: (i, 0)),
        out_shape=jax.ShapeDtypeStruct((m, d), F32),
        compiler_params=_params("parallel"),
        name="final_norm",
    )(x2, g)


def kernel(x, mem, norm_mix, w_in, diff_lq1, diff_lk1, diff_lq2, diff_lk2, diff_subln, lru_conv_w, lru_conv_b, lru_wa, lru_ba, lru_wx, lru_bx, lru_lambda, w_branch, w_out, norm_xattn, norm_mem, xa_wq, xa_wkv, xa_wo, norm_mlp, mlp_w1, mlp_w2, norm_final):
    bn, s_len, d = x.shape
    m_len = mem.shape[1]
    depth = norm_mix.shape[0]
    rows = bn * s_len

    qk0, qk1 = 3 * PROJ_TN, 4 * PROJ_TN
    w_qk = w_in[:, :, qk0:qk1].reshape(depth, d, (qk1 - qk0) // RET_KEY_DIM, RET_KEY_DIM // 2, 2)
    w_qk = jnp.swapaxes(w_qk, -1, -2).reshape(depth, d, qk1 - qk0)
    w_in_b = jnp.concatenate([w_in[:, :, :qk0], w_qk, w_in[:, :, qk1:]], axis=-1).astype(BF16)
    w_branch_b, w_out_b = w_branch.astype(BF16), w_out.astype(BF16)
    wq_b, wkv_b, wo_b = xa_wq.astype(BF16), xa_wkv.astype(BF16), xa_wo.astype(BF16)
    w1_b, w2_b = mlp_w1.astype(BF16), mlp_w2.astype(BF16)
    wa_b, wx_b = lru_wa.astype(BF16), lru_wx.astype(BF16)
    row3 = lambda p: p.reshape(depth, 1, p.shape[-1])
    conv_w = lru_conv_w.reshape(depth, CONV_WIDTH, d)

    pos = jnp.arange(s_len, dtype=F32)
    angle = 1.0 / (10000.0 ** jnp.linspace(0.0, 1.0, RET_KEY_DIM // 2, dtype=F32))
    phase = pos[:, None] * angle[None, :]
    cos = jnp.concatenate([jnp.cos(phase), jnp.cos(phase)], axis=1)
    sin = jnp.concatenate([-jnp.sin(phase), jnp.sin(phase)], axis=1)
    log_g = jnp.log(1.0 - jnp.exp2(-5.0 - jnp.arange(RET_HEADS, dtype=F32)))
    lg = jnp.broadcast_to(log_g[:, None, None], (RET_HEADS, 1, RET_VAL_DIM))
    lam_inits = [0.8 - 0.6 * math.exp(-0.3 * l) for l in range(depth)]
    cst = jnp.zeros((depth, 1, 128), F32)
    cst = cst.at[:, 0, 0].set(jnp.asarray(lam_inits, F32))
    cst = cst.at[:, 0, 1].set(jnp.asarray([1.0 - v for v in lam_inits], F32))

    x2 = x.reshape(rows, d)
    mem2 = mem.reshape(bn * m_len, d)
    for l in range(depth):
        ub2, uf2 = _proj_in(x2, row3(norm_mix), w_in_b, l)
        ub3 = ub2.reshape(bn, s_len, ub2.shape[1])
        uf3 = uf2.reshape(bn, s_len, uf2.shape[1])
        od = _diff_attn(ub3, row3(diff_lq1), row3(diff_lk1), row3(diff_lq2), row3(diff_lk2), cst,
                        row3(diff_subln), l)
        o_r = _retention(ub3, uf3, lg, cos, sin)
        o_l = _rglru(uf3, conv_w, row3(lru_conv_b), wa_b, row3(lru_ba), wx_b, row3(lru_bx),
                     row3(lru_lambda), l)
        x2 = _merge(od.reshape(rows, d), o_r.reshape(rows, d), o_l.reshape(rows, d), uf2, x2,
                    w_branch_b, w_out_b, l)
        kv = _norm_matmul(mem2, row3(norm_mem), wkv_b, l, 2 * XA_WIDTH, BF16, tm=512, tn=2 * XA_WIDTH)
        x2 = _xattn(x2.reshape(bn, s_len, d), row3(norm_xattn), wq_b, kv.reshape(bn, m_len, 2 * XA_WIDTH),
                    wo_b, l).reshape(rows, d)
        x2 = _mlp(x2, row3(norm_mlp), w1_b, w2_b, l)
    return _final_norm(x2, norm_final.reshape(1, d)).reshape(bn, s_len, d)
```

```python
import functools
import math

import jax
import jax.numpy as jnp
from jax import lax
from jax.experimental import pallas as pl
from jax.experimental.pallas import tpu as pltpu

F32 = jnp.float32
BF16 = jnp.bfloat16

D_MODEL = 1024
DIFF_HEADS = 8
DIFF_HEAD_DIM = 64
DIFF_VAL_DIM = 2 * DIFF_HEAD_DIM
RET_HEADS = 4
RET_KEY_DIM = 128
RET_VAL_DIM = 256
RET_CHUNK = 128
LRU_BLOCKS = 8
LRU_BLOCK_W = 128
CONV_WIDTH = 4
LRU_C = 8.0
XA_HEADS = 4
XA_HEAD_DIM = 128
XA_WIDTH = XA_HEADS * XA_HEAD_DIM
D_FF = 4 * D_MODEL

PROJ_TN = 1024
PROJ_TILES = (("dq", "b"), ("dk", "b"), ("dv", "b"), ("rqk", "f"), ("rv", "b"), ("rg", "silu"),
              ("lx", "f"), ("ly", "gelu"), ("g0", "sigmoid"), ("g1", "sigmoid"), ("g2", "sigmoid"))
PROJ_KINDS = tuple(kind for _, kind in PROJ_TILES)


def _proj_col(name):
    names = [n for n, _ in PROJ_TILES]
    kind = PROJ_KINDS[names.index(name)]
    same = [n for n, k in PROJ_TILES if (k == "f") == (kind == "f")]
    return same.index(name) * PROJ_TN


COLB_DQ, COLB_DK, COLB_DV, COLB_RV = (_proj_col(n) for n in ("dq", "dk", "dv", "rv"))
COLB_RG, COLB_LY, COLB_GATES = (_proj_col(n) for n in ("rg", "ly", "g0"))
COLF_RQ, COLF_LX = _proj_col("rqk"), _proj_col("lx")
COLF_RK = COLF_RQ + RET_HEADS * RET_KEY_DIM

VMEM_LIMIT_BYTES = 56 * 1024 * 1024

ATT_TQ = 256
ATT_TK = 256
ATT_HEADS_PER_STEP = 8
RET_CHUNKS_PER_TRIP = 4
LRU_TS = 256
SUBLANES = 8


def _params(*sem):
    return pltpu.CompilerParams(dimension_semantics=sem, vmem_limit_bytes=VMEM_LIMIT_BYTES)


def _rms_rows(xf, g, eps):
    return xf * lax.rsqrt(jnp.mean(xf * xf, axis=-1, keepdims=True) + eps) * g


def _norm_matmul_kernel(x_ref, g_ref, w_ref, o_ref, h_ref):
    @pl.when(pl.program_id(1) == 0)
    def _():
        h_ref[...] = _rms_rows(x_ref[...], g_ref[...], 1e-6).astype(BF16)

    o_ref[...] = jnp.dot(h_ref[...], w_ref[...], preferred_element_type=F32).astype(o_ref.dtype)


def _norm_matmul(x2, g, w, layer, n_cols, out_dtype, tm, tn):
    m, d = x2.shape
    tm = min(tm, m)
    return pl.pallas_call(
        _norm_matmul_kernel,
        grid=(m // tm, n_cols // tn),
        in_specs=[
            pl.BlockSpec((tm, d), lambda i, j: (i, 0)),
            pl.BlockSpec((None, 1, d), lambda i, j: (layer, 0, 0)),
            pl.BlockSpec((None, d, tn), lambda i, j: (layer, 0, j)),
        ],
        out_specs=pl.BlockSpec((tm, tn), lambda i, j: (i, j)),
        out_shape=jax.ShapeDtypeStruct((m, n_cols), out_dtype),
        scratch_shapes=[pltpu.VMEM((tm, d), BF16)],
        compiler_params=_params("parallel", "arbitrary"),
        name="norm_matmul",
    )(x2, g, w)


def _held_index(is_f32):
    table, n = [], 0
    for kind in PROJ_KINDS:
        if (kind == "f") == is_f32:
            n += 1
        table.append(max(n - 1, 0))
    steps = [j for j in range(1, len(table)) if table[j] != table[j - 1]]
    return lambda j: sum((j >= s).astype(jnp.int32) for s in steps)


def _gelu_tanh(y):
    return 0.5 * y * (1.0 + jnp.tanh(math.sqrt(2.0 / math.pi) * (y + 0.044715 * (y * y * y))))


_PROJ_EPILOGUES = {
    "f": lambda u: u,
    "b": lambda u: u,
    "silu": lambda u: u * jax.nn.sigmoid(u),
    "gelu": _gelu_tanh,
    "sigmoid": jax.nn.sigmoid,
}


def _proj_in_kernel(x_ref, g_ref, w_ref, ob_ref, of_ref, h_ref):
    j = pl.program_id(1)

    @pl.when(j == 0)
    def _():
        h_ref[...] = _rms_rows(x_ref[...], g_ref[...], 1e-6).astype(BF16)

    for kind in sorted(set(PROJ_KINDS)):
        is_kind = functools.reduce(jnp.logical_or, [j == t for t, k in enumerate(PROJ_KINDS) if k == kind])
        out_ref = of_ref if kind == "f" else ob_ref

        @pl.when(is_kind)
        def _(kind=kind, out_ref=out_ref):
            u = jnp.dot(h_ref[...], w_ref[...], preferred_element_type=F32)
            out_ref[...] = _PROJ_EPILOGUES[kind](u).astype(out_ref.dtype)


def _proj_in(x2, g, w, layer, tm=1024):
    m, d = x2.shape
    tm = min(tm, m)
    tn = PROJ_TN
    idx_b, idx_f = _held_index(False), _held_index(True)
    n_f32 = PROJ_KINDS.count("f")
    return pl.pallas_call(
        _proj_in_kernel,
        grid=(m // tm, len(PROJ_KINDS)),
        in_specs=[
            pl.BlockSpec((tm, d), lambda i, j: (i, 0)),
            pl.BlockSpec((None, 1, d), lambda i, j: (layer, 0, 0)),
            pl.BlockSpec((None, d, tn), lambda i, j: (layer, 0, j)),
        ],
        out_specs=[
            pl.BlockSpec((tm, tn), lambda i, j: (i, idx_b(j))),
            pl.BlockSpec((tm, tn), lambda i, j: (i, idx_f(j))),
        ],
        out_shape=[
            jax.ShapeDtypeStruct((m, tn * (len(PROJ_KINDS) - n_f32)), BF16),
            jax.ShapeDtypeStruct((m, tn * n_f32), F32),
        ],
        scratch_shapes=[pltpu.VMEM((tm, d), BF16)],
        compiler_params=_params("parallel", "arbitrary"),
        name="proj_in",
    )(x2, g, w)


def _diff_attn_kernel(lq1_ref, lk1_ref, lq2_ref, lk2_ref, cst_ref, subln_ref, q_ref, k_ref, v_ref,
                      o_ref, qq_ref, vt_ref, s_ref, m_ref, l_ref, acc_ref):
    i = pl.program_id(2)
    tq, tk, wv = ATT_TQ, ATT_TK, DIFF_VAL_DIM
    n_kv = k_ref.shape[0] // tk
    heads = range(ATT_HEADS_PER_STEP)

    @pl.when(i == 0)
    def _():
        for hd in heads:
            for t in range(n_kv):
                v_t = v_ref[t * tk:(t + 1) * tk, hd * wv:(hd + 1) * wv]
                vt_ref[hd, t] = v_t.astype(F32).T.astype(BF16)

    lane = lax.broadcasted_iota(jnp.int32, (tq, wv), 1)
    for hd in heads:
        q = q_ref[:, hd * wv:(hd + 1) * wv].astype(F32) * (DIFF_HEAD_DIM ** -0.5)
        qq_ref[hd, 0:tq, :] = jnp.where(lane < DIFF_HEAD_DIM, q, 0.0).astype(BF16)
        qq_ref[hd, tq:2 * tq, :] = jnp.where(lane >= DIFF_HEAD_DIM, q, 0.0).astype(BF16)

    m_ref[...] = jnp.full(m_ref.shape, -jnp.inf, F32)
    l_ref[...] = jnp.zeros(l_ref.shape, F32)
    acc_ref[...] = jnp.zeros(acc_ref.shape, F32)

    def scores(j, hd, masked):
        k_t = k_ref[pl.ds(pl.multiple_of(j * tk, tk), tk), hd * wv:(hd + 1) * wv]
        s = lax.dot_general(k_t, qq_ref[hd], (((1,), (1,)), ((), ())), preferred_element_type=F32)
        if masked:
            key = lax.broadcasted_iota(jnp.int32, (tk, tq), 0)
            qry = lax.broadcasted_iota(jnp.int32, (tk, tq), 1)
            ok = key <= qry
            s = jnp.where(jnp.concatenate([ok, ok], axis=1), s, -jnp.inf)
        s_ref[hd] = s

    def accumulate(j, hd):
        s = s_ref[hd]
        m_old = m_ref[hd]
        m_new = jnp.maximum(m_old, jnp.max(s, axis=0, keepdims=True))
        alpha = jnp.exp(m_old - m_new)
        e = jnp.exp(s - m_new)
        l_ref[hd] = alpha * l_ref[hd] + jnp.sum(e, axis=0, keepdims=True)
        acc_ref[hd] = alpha * acc_ref[hd] + jnp.dot(vt_ref[hd, j], e.astype(BF16), preferred_element_type=F32)
        m_ref[hd] = m_new

    def tiles(j, masked):
        for hd in heads:
            scores(j, hd, masked)
        for hd in heads:
            accumulate(j, hd)

    def body(j, carry):
        tiles(j, False)
        return carry

    lax.fori_loop(0, i, body, 0)
    tiles(i, True)

    lam_init = cst_ref[0:1, 0:1]
    out_scale = cst_ref[0:1, 1:2]
    lam = (jnp.exp(jnp.sum(lq1_ref[...] * lk1_ref[...], axis=-1, keepdims=True))
           - jnp.exp(jnp.sum(lq2_ref[...] * lk2_ref[...], axis=-1, keepdims=True)) + lam_init)
    for hd in heads:
        on = acc_ref[hd] * (1.0 / l_ref[hd])
        o = (on[:, :tq] - lam * on[:, tq:]).T
        o_ref[:, hd * wv:(hd + 1) * wv] = (_rms_rows(o, subln_ref[...], 1e-5) * out_scale).astype(o_ref.dtype)


def _diff_attn(ub3, lq1, lk1, lq2, lk2, cst, subln, layer):
    bn, s_len, _ = ub3.shape
    tq, hp = ATT_TQ, ATT_HEADS_PER_STEP
    assert s_len % tq == 0 and DIFF_HEADS % hp == 0
    nq = s_len // tq
    wv = DIFF_VAL_DIM
    wb = hp * wv
    vec = lambda width: pl.BlockSpec((None, 1, width), lambda b, h, i: (layer, 0, 0))
    return pl.pallas_call(
        _diff_attn_kernel,
        grid=(bn, DIFF_HEADS // hp, nq),
        in_specs=[
            vec(DIFF_HEAD_DIM), vec(DIFF_HEAD_DIM), vec(DIFF_HEAD_DIM), vec(DIFF_HEAD_DIM),
            vec(128), vec(wv),
            pl.BlockSpec((None, tq, wb), lambda b, h, i: (b, i, COLB_DQ // wb + h)),
            pl.BlockSpec((None, s_len, wb), lambda b, h, i: (b, 0, COLB_DK // wb + h)),
            pl.BlockSpec((None, s_len, wb), lambda b, h, i: (b, 0, COLB_DV // wb + h)),
        ],
        out_specs=pl.BlockSpec((None, tq, wb), lambda b, h, i: (b, i, h)),
        out_shape=jax.ShapeDtypeStruct((bn, s_len, DIFF_HEADS * wv), BF16),
        scratch_shapes=[
            pltpu.VMEM((hp, 2 * tq, wv), BF16),
            pltpu.VMEM((hp, s_len // ATT_TK, wv, ATT_TK), BF16),
            pltpu.VMEM((hp, ATT_TK, 2 * tq), F32),
            pltpu.VMEM((hp, 1, 2 * tq), F32),
            pltpu.VMEM((hp, 1, 2 * tq), F32),
            pltpu.VMEM((hp, wv, 2 * tq), F32),
        ],
        compiler_params=_params("parallel", "parallel", "arbitrary"),
        name="diff_attn",
    )(lq1, lk1, lq2, lk2, cst, subln, ub3, ub3, ub3)


def _retention_kernel(lg_ref, cos_ref, sin_ref, q_ref, k_ref, v_ref, g_ref, o_ref, state_ref):
    c = RET_CHUNK
    dk, dv = RET_KEY_DIM, RET_VAL_DIM
    n_chunks = q_ref.shape[0] // c
    lg = lg_ref[...]
    row = lax.broadcasted_iota(jnp.int32, (c, c), 0).astype(F32)
    col = lax.broadcasted_iota(jnp.int32, (c, c), 1).astype(F32)
    rel = row - col
    decay = jnp.where(rel >= 0, jnp.exp(jnp.maximum(rel, 0.0) * lg[:, :c]), 0.0)
    w_k = jnp.exp((c - 1.0 - row) * lg[:, :c])
    row_v = lax.broadcasted_iota(jnp.int32, (c, dv), 0).astype(F32)
    w_q = jnp.exp((row_v + 1.0) * lg)
    chunk_decay = jnp.exp(float(c) * lg)

    def rotary(t, cos, sin):
        return t * cos + pltpu.roll(t, dk // 2, axis=1) * sin

    state_ref[...] = jnp.zeros(state_ref.shape, F32)

    def chunk(n):
        rows = pl.ds(pl.multiple_of(n * c, c), c)
        cos, sin = cos_ref[rows, :], sin_ref[rows, :]
        qr = rotary(q_ref[rows, :], cos, sin)
        kr = rotary(k_ref[rows, :], cos, sin) * (dk ** -0.5)
        qb = qr.astype(BF16)
        vb = v_ref[rows, :]
        s = lax.dot_general(qb, kr.astype(BF16), (((1,), (1,)), ((), ())), preferred_element_type=F32) * decay
        state = state_ref[...]
        o = (jnp.dot(s.astype(BF16), vb, preferred_element_type=F32)
             + jnp.dot(qb, state.astype(BF16), preferred_element_type=F32) * w_q)
        kv = jnp.dot((kr * w_k).T.astype(BF16), vb, preferred_element_type=F32)
        state_ref[...] = chunk_decay * state + kv
        mu = jnp.mean(o, axis=-1, keepdims=True)
        var = jnp.mean(jnp.square(o - mu), axis=-1, keepdims=True)
        on = (o - mu) * lax.rsqrt(var + 1e-5)
        o_ref[rows, :] = (g_ref[rows, :].astype(F32) * on).astype(o_ref.dtype)

    def body(nn, carry):
        for t in range(RET_CHUNKS_PER_TRIP):
            chunk(nn * RET_CHUNKS_PER_TRIP + t)
        return carry

    lax.fori_loop(0, n_chunks // RET_CHUNKS_PER_TRIP, body, 0)


def _retention(ub3, uf3, lg, cos, sin):
    bn, s_len, _ = uf3.shape
    dk, dv = RET_KEY_DIM, RET_VAL_DIM
    assert s_len % (RET_CHUNK * RET_CHUNKS_PER_TRIP) == 0
    return pl.pallas_call(
        _retention_kernel,
        grid=(bn, RET_HEADS),
        in_specs=[
            pl.BlockSpec((None, 1, dv), lambda b, h: (h, 0, 0)),
            pl.BlockSpec((s_len, dk), lambda b, h: (0, 0)),
            pl.BlockSpec((s_len, dk), lambda b, h: (0, 0)),
            pl.BlockSpec((None, s_len, dk), lambda b, h: (b, 0, COLF_RQ // dk + h)),
            pl.BlockSpec((None, s_len, dk), lambda b, h: (b, 0, COLF_RK // dk + h)),
            pl.BlockSpec((None, s_len, dv), lambda b, h: (b, 0, COLB_RV // dv + h)),
            pl.BlockSpec((None, s_len, dv), lambda b, h: (b, 0, COLB_RG // dv + h)),
        ],
        out_specs=pl.BlockSpec((None, s_len, dv), lambda b, h: (b, 0, h)),
        out_shape=jax.ShapeDtypeStruct((bn, s_len, RET_HEADS * dv), BF16),
        scratch_shapes=[pltpu.VMEM((dk, dv), F32)],
        compiler_params=_params("parallel", "parallel"),
        name="retention",
    )(lg, cos, sin, uf3, uf3, ub3, ub3)


def _softplus(z):
    return jnp.maximum(z, 0.0) + jnp.log1p(jnp.exp(-jnp.abs(z)))


def _rglru_kernel(cw_ref, cb_ref, wa_ref, ba_ref, wx_ref, bx_ref, lam_ref, x_ref, y_ref, o_ref,
                  xpad_ref, h_ref):
    ts = x_ref.shape[0]
    halo = SUBLANES
    groups = ts // SUBLANES

    @pl.when(pl.program_id(1) == 0)
    def _():
        xpad_ref[0:halo, :] = jnp.zeros((halo, xpad_ref.shape[1]), F32)
        h_ref[...] = jnp.zeros(h_ref.shape, F32)

    @pl.when(pl.program_id(1) > 0)
    def _():
        xpad_ref[0:halo, :] = xpad_ref[ts:ts + halo, :]

    xpad_ref[halo:halo + ts, :] = x_ref[...]
    sub = lax.broadcasted_iota(jnp.int32, (groups, SUBLANES, LRU_BLOCK_W), 1)

    for n in range(LRU_BLOCKS):
        cols = slice(n * LRU_BLOCK_W, (n + 1) * LRU_BLOCK_W)
        xc = cb_ref[:, cols]
        for j in range(CONV_WIDTH):
            off = halo + j - (CONV_WIDTH - 1)
            xc = xc + cw_ref[j:j + 1, cols] * xpad_ref[off:off + ts, cols]
        xb = xc.astype(BF16)
        r = jax.nn.sigmoid(jnp.dot(xb, wa_ref[n], preferred_element_type=F32) + ba_ref[:, cols])
        gi = jax.nn.sigmoid(jnp.dot(xb, wx_ref[n], preferred_element_type=F32) + bx_ref[:, cols])
        log_a = (-LRU_C * _softplus(-lam_ref[:, cols])) * r
        a = jnp.exp(log_a)
        b = jnp.sqrt(1.0 - a * a) * (gi * xc)

        a3 = a.reshape(groups, SUBLANES, LRU_BLOCK_W)
        b3 = b.reshape(groups, SUBLANES, LRU_BLOCK_W)
        d = 1
        while d < SUBLANES:
            keep = sub >= d
            b3 = jnp.where(keep, a3 * pltpu.roll(b3, d, axis=1) + b3, b3)
            a3 = jnp.where(keep, a3 * pltpu.roll(a3, d, axis=1), a3)
            d *= 2
        h = h_ref[:, cols]
        hs = []
        for g in range(groups):
            hg = b3[g] + a3[g] * h
            h = hg[SUBLANES - 1:SUBLANES, :]
            hs.append(hg)
        h_ref[:, cols] = h
        o_ref[:, cols] = (jnp.concatenate(hs, axis=0) * y_ref[:, cols].astype(F32)).astype(o_ref.dtype)


def _rglru(ub3, uf3, conv_w, conv_b, wa, ba, wx, bx, lam, layer):
    bn, s_len, _ = uf3.shape
    w = LRU_BLOCKS * LRU_BLOCK_W
    ts = min(LRU_TS, s_len)
    vec = lambda: pl.BlockSpec((None, 1, w), lambda b, t: (layer, 0, 0))
    blk = lambda: pl.BlockSpec((None, LRU_BLOCKS, LRU_BLOCK_W, LRU_BLOCK_W), lambda b, t: (layer, 0, 0, 0))
    return pl.pallas_call(
        _rglru_kernel,
        grid=(bn, s_len // ts),
        in_specs=[
            pl.BlockSpec((None, CONV_WIDTH, w), lambda b, t: (layer, 0, 0)),
            vec(), blk(), vec(), blk(), vec(), vec(),
            pl.BlockSpec((None, ts, w), lambda b, t: (b, t, COLF_LX // w)),
            pl.BlockSpec((None, ts, w), lambda b, t: (b, t, COLB_LY // w)),
        ],
        out_specs=pl.BlockSpec((None, ts, w), lambda b, t: (b, t, 0)),
        out_shape=jax.ShapeDtypeStruct((bn, s_len, w), BF16),
        scratch_shapes=[pltpu.VMEM((ts + SUBLANES, w), F32), pltpu.VMEM((1, w), F32)],
        compiler_params=_params("parallel", "arbitrary"),
        name="rglru",
    )(conv_w, conv_b, wa, ba, wx, bx, lam, uf3, ub3)


def _merge_kernel(od_ref, or_ref, ol_ref, g0_ref, g1_ref, g2_ref, x_ref, wb_ref, wo_ref, o_ref):
    merged = (g0_ref[...].astype(F32) * jnp.dot(od_ref[...], wb_ref[0], preferred_element_type=F32)
              + g1_ref[...].astype(F32) * jnp.dot(or_ref[...], wb_ref[1], preferred_element_type=F32)
              + g2_ref[...].astype(F32) * jnp.dot(ol_ref[...], wb_ref[2], preferred_element_type=F32))
    o_ref[...] = x_ref[...] + jnp.dot(merged.astype(BF16), wo_ref[...], preferred_element_type=F32)


def _merge(od, o_r, o_l, u2, x2, w_branch, w_out, layer, tm=256):
    m, d = x2.shape
    tm = min(tm, m)
    row = lambda: pl.BlockSpec((tm, d), lambda i: (i, 0))
    gate = lambda k: pl.BlockSpec((tm, d), lambda i: (i, COLB_GATES // d + k))
    return pl.pallas_call(
        _merge_kernel,
        grid=(m // tm,),
        in_specs=[
            row(), row(), row(), gate(0), gate(1), gate(2), row(),
            pl.BlockSpec((None, 3, d, d), lambda i: (layer, 0, 0, 0)),
            pl.BlockSpec((None, d, d), lambda i: (layer, 0, 0)),
        ],
        out_specs=row(),
        out_shape=jax.ShapeDtypeStruct((m, d), F32),
        compiler_params=_params("parallel"),
        name="merge",
    )(od, o_r, o_l, u2, u2, u2, x2, w_branch, w_out)


def _xattn_kernel(x_ref, g_ref, wq_ref, kv_ref, wo_ref, o_ref):
    x = x_ref[...]
    h = _rms_rows(x, g_ref[...], 1e-6).astype(BF16)
    q = jnp.dot(h, wq_ref[...], preferred_element_type=F32).astype(BF16)
    heads = []
    for hd in range(XA_HEADS):
        kc = slice(hd * XA_HEAD_DIM, (hd + 1) * XA_HEAD_DIM)
        vc = slice(XA_WIDTH + hd * XA_HEAD_DIM, XA_WIDTH + (hd + 1) * XA_HEAD_DIM)
        s = lax.dot_general(q[:, kc], kv_ref[:, kc], (((1,), (1,)), ((), ())),
                            preferred_element_type=F32) * (XA_HEAD_DIM ** -0.5)
        e = jnp.exp(s - jnp.max(s, axis=-1, keepdims=True))
        pv = jnp.dot(e.astype(BF16), kv_ref[:, vc], preferred_element_type=F32)
        heads.append((pv / jnp.sum(e, axis=-1, keepdims=True)).astype(BF16))
    o = jnp.concatenate(heads, axis=-1)
    o_ref[...] = x + jnp.dot(o, wo_ref[...], preferred_element_type=F32)


def _xattn(x3, g, wq, kv3, wo, layer, tm=512):
    bn, s_len, d = x3.shape
    tm = min(tm, s_len)
    m_len = kv3.shape[1]
    return pl.pallas_call(
        _xattn_kernel,
        grid=(bn, s_len // tm),
        in_specs=[
            pl.BlockSpec((None, tm, d), lambda b, i: (b, i, 0)),
            pl.BlockSpec((None, 1, d), lambda b, i: (layer, 0, 0)),
            pl.BlockSpec((None, d, XA_WIDTH), lambda b, i: (layer, 0, 0)),
            pl.BlockSpec((None, m_len, 2 * XA_WIDTH), lambda b, i: (b, 0, 0)),
            pl.BlockSpec((None, XA_WIDTH, d), lambda b, i: (layer, 0, 0)),
        ],
        out_specs=pl.BlockSpec((None, tm, d), lambda b, i: (b, i, 0)),
        out_shape=jax.ShapeDtypeStruct((bn, s_len, d), F32),
        compiler_params=_params("parallel", "parallel"),
        name="xattn",
    )(x3, g, wq, kv3, wo)


def _mlp_kernel(x_ref, g_ref, w1_ref, w2_ref, gf_ref, o_ref, *, final_norm):
    x = x_ref[...]
    h = _rms_rows(x, g_ref[...], 1e-6).astype(BF16)
    d = x.shape[1]
    acc = x
    for c in range(w1_ref.shape[1] // d):
        cols = slice(c * d, (c + 1) * d)
        a = jnp.maximum(jnp.dot(h, w1_ref[:, cols], preferred_element_type=F32), 0.0)
        acc = acc + jnp.dot((a * a).astype(BF16), w2_ref[cols, :], preferred_element_type=F32)
    o_ref[...] = _rms_rows(acc, gf_ref[...], 1e-6) if final_norm else acc


def _mlp(x2, g, w1, w2, g_final, layer, final_norm, tm=512):
    m, d = x2.shape
    tm = min(tm, m)
    dff = w1.shape[2]
    return pl.pallas_call(
        functools.partial(_mlp_kernel, final_norm=final_norm),
        grid=(m // tm,),
        in_specs=[
            pl.BlockSpec((tm, d), lambda i: (i, 0)),
            pl.BlockSpec((None, 1, d), lambda i: (layer, 0, 0)),
            pl.BlockSpec((None, d, dff), lambda i: (layer, 0, 0)),
            pl.BlockSpec((None, dff, d), lambda i: (layer, 0, 0)),
            pl.BlockSpec((1, d), lambda i: (0, 0)),
        ],
        out_specs=pl.BlockSpec((tm, d), lambda i: (i, 0)),
        out_shape=jax.ShapeDtypeStruct((m, d), F32),
        compiler_params=_params("parallel"),
        name="mlp",
    )(x2, g, w1, w2, g_final)


def kernel(x, mem, norm_mix, w_in, diff_lq1, diff_lk1, diff_lq2, diff_lk2, diff_subln, lru_conv_w, lru_conv_b, lru_wa, lru_ba, lru_wx, lru_bx, lru_lambda, w_branch, w_out, norm_xattn, norm_mem, xa_wq, xa_wkv, xa_wo, norm_mlp, mlp_w1, mlp_w2, norm_final):
    bn, s_len, d = x.shape
    m_len = mem.shape[1]
    depth = norm_mix.shape[0]
    rows = bn * s_len

    qk0, qk1 = 3 * PROJ_TN, 4 * PROJ_TN
    w_qk = w_in[:, :, qk0:qk1].reshape(depth, d, (qk1 - qk0) // RET_KEY_DIM, RET_KEY_DIM // 2, 2)
    w_qk = jnp.swapaxes(w_qk, -1, -2).reshape(depth, d, qk1 - qk0)
    w_in_b = jnp.concatenate([w_in[:, :, :qk0], w_qk, w_in[:, :, qk1:]], axis=-1).astype(BF16)
    w_branch_b, w_out_b = w_branch.astype(BF16), w_out.astype(BF16)
    wq_b, wkv_b, wo_b = xa_wq.astype(BF16), xa_wkv.astype(BF16), xa_wo.astype(BF16)
    w1_b, w2_b = mlp_w1.astype(BF16), mlp_w2.astype(BF16)
    wa_b, wx_b = lru_wa.astype(BF16), lru_wx.astype(BF16)
    row3 = lambda p: p.reshape(depth, 1, p.shape[-1])
    conv_w = lru_conv_w.reshape(depth, CONV_WIDTH, d)

    pos = jnp.arange(s_len, dtype=F32)
    angle = 1.0 / (10000.0 ** jnp.linspace(0.0, 1.0, RET_KEY_DIM // 2, dtype=F32))
    phase = pos[:, None] * angle[None, :]
    cos = jnp.concatenate([jnp.cos(phase), jnp.cos(phase)], axis=1)
    sin = jnp.concatenate([-jnp.sin(phase), jnp.sin(phase)], axis=1)
    log_g = jnp.log(1.0 - jnp.exp2(-5.0 - jnp.arange(RET_HEADS, dtype=F32)))
    lg = jnp.broadcast_to(log_g[:, None, None], (RET_HEADS, 1, RET_VAL_DIM))
    lam_inits = [0.8 - 0.6 * math.exp(-0.3 * l) for l in range(depth)]
    cst = jnp.zeros((depth, 1, 128), F32)
    cst = cst.at[:, 0, 0].set(jnp.asarray(lam_inits, F32))
    cst = cst.at[:, 0, 1].set(jnp.asarray([1.0 - v for v in lam_inits], F32))

    x2 = x.reshape(rows, d)
    mem2 = mem.reshape(bn * m_len, d)
    for l in range(depth):
        ub2, uf2 = _proj_in(x2, row3(norm_mix), w_in_b, l)
        ub3 = ub2.reshape(bn, s_len, ub2.shape[1])
        uf3 = uf2.reshape(bn, s_len, uf2.shape[1])
        od = _diff_attn(ub3, row3(diff_lq1), row3(diff_lk1), row3(diff_lq2), row3(diff_lk2), cst,
                        row3(diff_subln), l)
        o_r = _retention(ub3, uf3, lg, cos, sin)
        o_l = _rglru(ub3, uf3, conv_w, row3(lru_conv_b), wa_b, row3(lru_ba), wx_b, row3(lru_bx),
                     row3(lru_lambda), l)
        x2 = _merge(od.reshape(rows, d), o_r.reshape(rows, d), o_l.reshape(rows, d), ub2, x2,
                    w_branch_b, w_out_b, l)
        kv = _norm_matmul(mem2, row3(norm_mem), wkv_b, l, 2 * XA_WIDTH, BF16, tm=512, tn=2 * XA_WIDTH)
        x2 = _xattn(x2.reshape(bn, s_len, d), row3(norm_xattn), wq_b, kv.reshape(bn, m_len, 2 * XA_WIDTH),
                    wo_b, l).reshape(rows, d)
        x2 = _mlp(x2, row3(norm_mlp), w1_b, w2_b, norm_final.reshape(1, d), l, final_norm=(l == depth - 1))
    return x2.reshape(bn, s_len, d)
```

```python
import functools
import math

import jax
import jax.numpy as jnp
from jax import lax
from jax.experimental import pallas as pl
from jax.experimental.pallas import tpu as pltpu

F32 = jnp.float32
BF16 = jnp.bfloat16

D_MODEL = 1024
DIFF_HEADS = 8
DIFF_HEAD_DIM = 64
DIFF_VAL_DIM = 2 * DIFF_HEAD_DIM
RET_HEADS = 4
RET_KEY_DIM = 128
RET_VAL_DIM = 256
RET_CHUNK = 128
LRU_BLOCKS = 8
LRU_BLOCK_W = 128
CONV_WIDTH = 4
LRU_C = 8.0
XA_HEADS = 4
XA_HEAD_DIM = 128
XA_WIDTH = XA_HEADS * XA_HEAD_DIM
D_FF = 4 * D_MODEL

PROJ_TN = 1024
PROJ_CHUNK = 256
PROJ_TILES =(("dq", "b"), ("dk", "b"), ("dv", "b"), ("rqk", "f"), ("rv", "b"), ("rg", "silu"),
              ("lx", "f"), ("ly", "gelu"), ("g0", "sigmoid"), ("g1", "sigmoid"), ("g2", "sigmoid"))
PROJ_KINDS = tuple(kind for _, kind in PROJ_TILES)


def _proj_col(name):
    names = [n for n, _ in PROJ_TILES]
    kind = PROJ_KINDS[names.index(name)]
    same = [n for n, k in PROJ_TILES if (k == "f") == (kind == "f")]
    return same.index(name) * PROJ_TN


COLB_DQ, COLB_DK, COLB_DV, COLB_RV = (_proj_col(n) for n in ("dq", "dk", "dv", "rv"))
COLB_RG, COLB_LY, COLB_GATES = (_proj_col(n) for n in ("rg", "ly", "g0"))
COLF_RQ, COLF_LX = _proj_col("rqk"), _proj_col("lx")
COLF_RK = COLF_RQ + RET_HEADS * RET_KEY_DIM

VMEM_LIMIT_BYTES = 56 * 1024 * 1024

ATT_TQ = 256
ATT_TK = 256
ATT_HEADS_PER_STEP = 8
RET_CHUNKS_PER_TRIP = 4
RET_HEADS_PER_STEP = 2
LRU_TS = 256
SUBLANES = 8


def _params(*sem):
    return pltpu.CompilerParams(dimension_semantics=sem, vmem_limit_bytes=VMEM_LIMIT_BYTES)


def _rms_rows(xf, g, eps):
    return xf * lax.rsqrt(jnp.mean(xf * xf, axis=-1, keepdims=True) + eps) * g


def _norm_matmul_kernel(x_ref, g_ref, w_ref, o_ref, h_ref):
    @pl.when(pl.program_id(1) == 0)
    def _():
        h_ref[...] = _rms_rows(x_ref[...], g_ref[...], 1e-6).astype(BF16)

    o_ref[...] = jnp.dot(h_ref[...], w_ref[...], preferred_element_type=F32).astype(o_ref.dtype)


def _norm_matmul(x2, g, w, layer, n_cols, out_dtype, tm, tn):
    m, d = x2.shape
    tm = min(tm, m)
    return pl.pallas_call(
        _norm_matmul_kernel,
        grid=(m // tm, n_cols // tn),
        in_specs=[
            pl.BlockSpec((tm, d), lambda i, j: (i, 0)),
            pl.BlockSpec((None, 1, d), lambda i, j: (layer, 0, 0)),
            pl.BlockSpec((None, d, tn), lambda i, j: (layer, 0, j)),
        ],
        out_specs=pl.BlockSpec((tm, tn), lambda i, j: (i, j)),
        out_shape=jax.ShapeDtypeStruct((m, n_cols), out_dtype),
        scratch_shapes=[pltpu.VMEM((tm, d), BF16)],
        compiler_params=_params("parallel", "arbitrary"),
        name="norm_matmul",
    )(x2, g, w)


def _held_index(is_f32):
    table, n = [], 0
    for kind in PROJ_KINDS:
        if (kind == "f") == is_f32:
            n += 1
        table.append(max(n - 1, 0))
    steps = [j for j in range(1, len(table)) if table[j] != table[j - 1]]
    return lambda j: sum((j >= s).astype(jnp.int32) for s in steps)


def _gelu_tanh(y):
    return 0.5 * y * (1.0 + jnp.tanh(math.sqrt(2.0 / math.pi) * (y + 0.044715 * (y * y * y))))


_PROJ_EPILOGUES = {
    "f": lambda u: u,
    "b": lambda u: u,
    "silu": lambda u: u * jax.nn.sigmoid(u),
    "gelu": _gelu_tanh,
    "sigmoid": jax.nn.sigmoid,
}


def _proj_in_kernel(x_ref, g_ref, w_ref, ob_ref, of_ref, h_ref):
    j = pl.program_id(1)

    @pl.when(j == 0)
    def _():
        h_ref[...] = _rms_rows(x_ref[...], g_ref[...], 1e-6).astype(BF16)

    for kind in sorted(set(PROJ_KINDS)):
        is_kind = functools.reduce(jnp.logical_or, [j == t for t, k in enumerate(PROJ_KINDS) if k == kind])
        out_ref = of_ref if kind == "f" else ob_ref

        @pl.when(is_kind)
        def _(kind=kind, out_ref=out_ref):
            for c in range(0, PROJ_TN, PROJ_CHUNK):
                u = jnp.dot(h_ref[...], w_ref[:, c:c + PROJ_CHUNK], preferred_element_type=F32)
                out_ref[:, c:c + PROJ_CHUNK] = _PROJ_EPILOGUES[kind](u).astype(out_ref.dtype)


def _proj_in(x2, g, w, layer, tm=2048):
    m, d = x2.shape
    tm = min(tm, m)
    tn = PROJ_TN
    idx_b, idx_f = _held_index(False), _held_index(True)
    n_f32 = PROJ_KINDS.count("f")
    return pl.pallas_call(
        _proj_in_kernel,
        grid=(m // tm, len(PROJ_KINDS)),
        in_specs=[
            pl.BlockSpec((tm, d), lambda i, j: (i, 0)),
            pl.BlockSpec((None, 1, d), lambda i, j: (layer, 0, 0)),
            pl.BlockSpec((None, d, tn), lambda i, j: (layer, 0, j)),
        ],
        out_specs=[
            pl.BlockSpec((tm, tn), lambda i, j: (i, idx_b(j))),
            pl.BlockSpec((tm, tn), lambda i, j: (i, idx_f(j))),
        ],
        out_shape=[
            jax.ShapeDtypeStruct((m, tn * (len(PROJ_KINDS) - n_f32)), BF16),
            jax.ShapeDtypeStruct((m, tn * n_f32), F32),
        ],
        scratch_shapes=[pltpu.VMEM((tm, d), BF16)],
        compiler_params=_params("parallel", "arbitrary"),
        name="proj_in",
    )(x2, g, w)


def _diff_attn_kernel(lq1_ref, lk1_ref, lq2_ref, lk2_ref, cst_ref, subln_ref, q_ref, k_ref, v_ref,
                      o_ref, qq_ref, vt_ref, s_ref, m_ref, l_ref, acc_ref):
    i = pl.program_id(2)
    tq, tk, wv = ATT_TQ, ATT_TK, DIFF_VAL_DIM
    n_kv = k_ref.shape[0] // tk
    heads = range(ATT_HEADS_PER_STEP)

    @pl.when(i == 0)
    def _():
        for hd in heads:
            for t in range(n_kv):
                v_t = v_ref[t * tk:(t + 1) * tk, hd * wv:(hd + 1) * wv]
                vt_ref[hd, t] = v_t.astype(F32).T.astype(BF16)

    lane = lax.broadcasted_iota(jnp.int32, (tq, wv), 1)
    for hd in heads:
        q = q_ref[:, hd * wv:(hd + 1) * wv].astype(F32)
        qq_ref[hd, 0:tq, :] = jnp.where(lane < DIFF_HEAD_DIM, q, 0.0).astype(BF16)
        qq_ref[hd, tq:2 * tq, :] = jnp.where(lane >= DIFF_HEAD_DIM, q, 0.0).astype(BF16)

    m_ref[...] = jnp.full(m_ref.shape, -jnp.inf, F32)
    l_ref[...] = jnp.zeros(l_ref.shape, F32)
    acc_ref[...] = jnp.zeros(acc_ref.shape, F32)

    def scores(j, hd):
        k_t = k_ref[pl.ds(pl.multiple_of(j * tk, tk), tk), hd * wv:(hd + 1) * wv]
        s_ref[hd] = lax.dot_general(k_t, qq_ref[hd], (((1,), (1,)), ((), ())),
                                    preferred_element_type=F32)

    def accumulate(j, hd, masked):
        s = s_ref[hd]
        if masked:
            key = lax.broadcasted_iota(jnp.int32, (tk, tq), 0)
            qry = lax.broadcasted_iota(jnp.int32, (tk, tq), 1)
            ok = key <= qry
            s = jnp.where(jnp.concatenate([ok, ok], axis=1), s, -jnp.inf)
        m_old = m_ref[hd]
        m_new = jnp.maximum(m_old, jnp.max(s, axis=0, keepdims=True))
        alpha = jnp.exp2(m_old - m_new)
        e = jnp.exp2(s - m_new)
        l_ref[hd] = alpha * l_ref[hd] + jnp.sum(e, axis=0, keepdims=True)
        acc_ref[hd] = alpha * acc_ref[hd] + jnp.dot(vt_ref[hd, j], e.astype(BF16), preferred_element_type=F32)
        m_ref[hd] = m_new

    for hd in heads:
        scores(0, hd)

    def body(j, carry):
        for hd in heads:
            accumulate(j, hd, False)
            scores(j + 1, hd)
        return carry

    lax.fori_loop(0, i, body, 0)
    for hd in heads:
        accumulate(i, hd, True)

    lam_init = cst_ref[0:1, 0:1]
    out_scale = cst_ref[0:1, 1:2]
    lam = (jnp.exp(jnp.sum(lq1_ref[...] * lk1_ref[...], axis=-1, keepdims=True))
           - jnp.exp(jnp.sum(lq2_ref[...] * lk2_ref[...], axis=-1, keepdims=True)) + lam_init)
    for hd in heads:
        on = acc_ref[hd] * (1.0 / l_ref[hd])
        o = (on[:, :tq] - lam * on[:, tq:]).T
        o_ref[:, hd * wv:(hd + 1) * wv] = (_rms_rows(o, subln_ref[...], 1e-5) * out_scale).astype(o_ref.dtype)


def _diff_attn(ub3, lq1, lk1, lq2, lk2, cst, subln, layer):
    bn, s_len, _ = ub3.shape
    tq, hp = ATT_TQ, ATT_HEADS_PER_STEP
    assert s_len % tq == 0 and DIFF_HEADS % hp == 0
    nq = s_len // tq
    wv = DIFF_VAL_DIM
    wb = hp * wv
    vec = lambda width: pl.BlockSpec((None, 1, width), lambda b, h, i: (layer, 0, 0))
    return pl.pallas_call(
        _diff_attn_kernel,
        grid=(bn, DIFF_HEADS // hp, nq),
        in_specs=[
            vec(DIFF_HEAD_DIM), vec(DIFF_HEAD_DIM), vec(DIFF_HEAD_DIM), vec(DIFF_HEAD_DIM),
            vec(128), vec(wv),
            pl.BlockSpec((None, tq, wb), lambda b, h, i: (b, i, COLB_DQ // wb + h)),
            pl.BlockSpec((None, s_len, wb), lambda b, h, i: (b, 0, COLB_DK // wb + h)),
            pl.BlockSpec((None, s_len, wb), lambda b, h, i: (b, 0, COLB_DV // wb + h)),
        ],
        out_specs=pl.BlockSpec((None, tq, wb), lambda b, h, i: (b, i, h)),
        out_shape=jax.ShapeDtypeStruct((bn, s_len, DIFF_HEADS * wv), BF16),
        scratch_shapes=[
            pltpu.VMEM((hp, 2 * tq, wv), BF16),
            pltpu.VMEM((hp, s_len // ATT_TK, wv, ATT_TK), BF16),
            pltpu.VMEM((hp, ATT_TK, 2 * tq), F32),
            pltpu.VMEM((hp, 1, 2 * tq), F32),
            pltpu.VMEM((hp, 1, 2 * tq), F32),
            pltpu.VMEM((hp, wv, 2 * tq), F32),
        ],
        compiler_params=_params("parallel", "parallel", "arbitrary"),
        name="diff_attn",
    )(lq1, lk1, lq2, lk2, cst, subln, ub3, ub3, ub3)


def _retention_kernel(lg_ref, cos_ref, sin_ref, q_ref, k_ref, v_ref, g_ref, o_ref, state_ref):
    c = RET_CHUNK
    dk, dv = RET_KEY_DIM, RET_VAL_DIM
    n_chunks = q_ref.shape[0] // c
    heads = range(RET_HEADS_PER_STEP)
    row = lax.broadcasted_iota(jnp.int32, (c, c), 0).astype(F32)
    col = lax.broadcasted_iota(jnp.int32, (c, c), 1).astype(F32)
    rel = row - col
    row_v = lax.broadcasted_iota(jnp.int32, (c, dv), 0).astype(F32)
    tables = []
    for hd in heads:
        lg = lg_ref[hd]
        decay = jnp.where(rel >= 0, jnp.exp(jnp.maximum(rel, 0.0) * lg[:, :c]), 0.0)
        w_k = jnp.exp((c - 1.0 - row) * lg[:, :c])
        w_q = jnp.exp((row_v + 1.0) * lg)
        chunk_decay = jnp.exp(float(c) * lg)
        tables.append((decay, w_k, w_q, chunk_decay))

    def rotary(t, cos, sin):
        return t * cos + pltpu.roll(t, dk // 2, axis=1) * sin

    state_ref[...] = jnp.zeros(state_ref.shape, F32)

    def body(nn, carry):
        work = {}
        for t in range(RET_CHUNKS_PER_TRIP):
            for hd in heads:
                _, w_k, _, _ = tables[hd]
                rows = pl.ds(pl.multiple_of((nn * RET_CHUNKS_PER_TRIP + t) * c, c), c)
                kcols = slice(hd * dk, (hd + 1) * dk)
                vcols = slice(hd * dv, (hd + 1) * dv)
                cos, sin = cos_ref[rows, :], sin_ref[rows, :]
                qb = rotary(q_ref[rows, kcols], cos, sin).astype(BF16)
                kr = rotary(k_ref[rows, kcols], cos, sin) * (dk ** -0.5)
                vb = v_ref[rows, vcols]
                s = lax.dot_general(qb, kr.astype(BF16), (((1,), (1,)), ((), ())), preferred_element_type=F32)
                kv = jnp.dot((kr * w_k).T.astype(BF16), vb, preferred_element_type=F32)
                work[t, hd] = (rows, vcols, qb, vb, s, kv)
        outs = {}
        for hd in heads:
            decay, _, w_q, chunk_decay = tables[hd]
            state = state_ref[hd]
            for t in range(RET_CHUNKS_PER_TRIP):
                rows, vcols, qb, vb, s, kv = work[t, hd]
                outs[t, hd] = (jnp.dot((s * decay).astype(BF16), vb, preferred_element_type=F32)
                               + jnp.dot(qb, state.astype(BF16), preferred_element_type=F32) * w_q)
                state = chunk_decay * state + kv
            state_ref[hd] = state
        for (t, hd), o in outs.items():
            rows, vcols = work[t, hd][:2]
            mu = jnp.mean(o, axis=-1, keepdims=True)
            var = jnp.mean(jnp.square(o - mu), axis=-1, keepdims=True)
            on = (o - mu) * lax.rsqrt(var + 1e-5)
            o_ref[rows, vcols] = (g_ref[rows, vcols].astype(F32) * on).astype(o_ref.dtype)
        return carry

    lax.fori_loop(0, n_chunks // RET_CHUNKS_PER_TRIP, body, 0)


def _retention(ub3, uf3, lg, cos, sin):
    bn, s_len, _ = uf3.shape
    dk, dv = RET_KEY_DIM, RET_VAL_DIM
    hp = RET_HEADS_PER_STEP
    assert s_len % (RET_CHUNK * RET_CHUNKS_PER_TRIP) == 0 and RET_HEADS % hp == 0
    wk, wv = hp * dk, hp * dv
    return pl.pallas_call(
        _retention_kernel,
        grid=(bn, RET_HEADS // hp),
        in_specs=[
            pl.BlockSpec((hp, 1, dv), lambda b, h: (h, 0, 0)),
            pl.BlockSpec((s_len, dk), lambda b, h: (0, 0)),
            pl.BlockSpec((s_len, dk), lambda b, h: (0, 0)),
            pl.BlockSpec((None, s_len, wk), lambda b, h: (b, 0, COLF_RQ // wk + h)),
            pl.BlockSpec((None, s_len, wk), lambda b, h: (b, 0, COLF_RK // wk + h)),
            pl.BlockSpec((None, s_len, wv), lambda b, h: (b, 0, COLB_RV // wv + h)),
            pl.BlockSpec((None, s_len, wv), lambda b, h: (b, 0, COLB_RG // wv + h)),
        ],
        out_specs=pl.BlockSpec((None, s_len, wv), lambda b, h: (b, 0, h)),
        out_shape=jax.ShapeDtypeStruct((bn, s_len, RET_HEADS * dv), BF16),
        scratch_shapes=[pltpu.VMEM((hp, dk, dv), F32)],
        compiler_params=_params("parallel", "parallel"),
        name="retention",
    )(lg, cos, sin, uf3, uf3, ub3, ub3)


def _softplus(z):
    return jnp.maximum(z, 0.0) + jnp.log1p(jnp.exp(-jnp.abs(z)))


def _rglru_kernel(cw_ref, cb_ref, wa_ref, ba_ref, wx_ref, bx_ref, lam_ref, x_ref, y_ref, o_ref,
                  xpad_ref, h_ref):
    ts = x_ref.shape[0]
    halo = SUBLANES
    groups = ts // SUBLANES

    @pl.when(pl.program_id(1) == 0)
    def _():
        xpad_ref[0:halo, :] = jnp.zeros((halo, xpad_ref.shape[1]), F32)
        h_ref[...] = jnp.zeros(h_ref.shape, F32)

    @pl.when(pl.program_id(1) > 0)
    def _():
        xpad_ref[0:halo, :] = xpad_ref[ts:ts + halo, :]

    xpad_ref[halo:halo + ts, :] = x_ref[...]
    sub = lax.broadcasted_iota(jnp.int32, (groups, SUBLANES, LRU_BLOCK_W), 1)

    for n in range(LRU_BLOCKS):
        cols = slice(n * LRU_BLOCK_W, (n + 1) * LRU_BLOCK_W)
        xc = cb_ref[:, cols]
        for j in range(CONV_WIDTH):
            off = halo + j - (CONV_WIDTH - 1)
            xc = xc + cw_ref[j:j + 1, cols] * xpad_ref[off:off + ts, cols]
        xb = xc.astype(BF16)
        r = jax.nn.sigmoid(jnp.dot(xb, wa_ref[n], preferred_element_type=F32) + ba_ref[:, cols])
        gi = jax.nn.sigmoid(jnp.dot(xb, wx_ref[n], preferred_element_type=F32) + bx_ref[:, cols])
        log_a = (-LRU_C * _softplus(-lam_ref[:, cols])) * r
        a = jnp.exp(log_a)
        b = jnp.sqrt(1.0 - a * a) * (gi * xc)

        a3 = a.reshape(groups, SUBLANES, LRU_BLOCK_W)
        b3 = b.reshape(groups, SUBLANES, LRU_BLOCK_W)
        d = 1
        while d < SUBLANES:
            keep = sub >= d
            b3 = jnp.where(keep, a3 * pltpu.roll(b3, d, axis=1) + b3, b3)
            a3 = jnp.where(keep, a3 * pltpu.roll(a3, d, axis=1), a3)
            d *= 2
        h = h_ref[:, cols]
        hs = []
        for g in range(groups):
            hg = b3[g] + a3[g] * h
            h = hg[SUBLANES - 1:SUBLANES, :]
            hs.append(hg)
        h_ref[:, cols] = h
        o_ref[:, cols] = (jnp.concatenate(hs, axis=0) * y_ref[:, cols].astype(F32)).astype(o_ref.dtype)


def _rglru(ub3, uf3, conv_w, conv_b, wa, ba, wx, bx, lam, layer):
    bn, s_len, _ = uf3.shape
    w = LRU_BLOCKS * LRU_BLOCK_W
    ts = min(LRU_TS, s_len)
    vec = lambda: pl.BlockSpec((None, 1, w), lambda b, t: (layer, 0, 0))
    blk = lambda: pl.BlockSpec((None, LRU_BLOCKS, LRU_BLOCK_W, LRU_BLOCK_W), lambda b, t: (layer, 0, 0, 0))
    return pl.pallas_call(
        _rglru_kernel,
        grid=(bn, s_len // ts),
        in_specs=[
            pl.BlockSpec((None, CONV_WIDTH, w), lambda b, t: (layer, 0, 0)),
            vec(), blk(), vec(), blk(), vec(), vec(),
            pl.BlockSpec((None, ts, w), lambda b, t: (b, t, COLF_LX // w)),
            pl.BlockSpec((None, ts, w), lambda b, t: (b, t, COLB_LY // w)),
        ],
        out_specs=pl.BlockSpec((None, ts, w), lambda b, t: (b, t, 0)),
        out_shape=jax.ShapeDtypeStruct((bn, s_len, w), BF16),
        scratch_shapes=[pltpu.VMEM((ts + SUBLANES, w), F32), pltpu.VMEM((1, w), F32)],
        compiler_params=_params("parallel", "arbitrary"),
        name="rglru",
    )(conv_w, conv_b, wa, ba, wx, bx, lam, uf3, ub3)


def _merge_kernel(od_ref, or_ref, ol_ref, g0_ref, g1_ref, g2_ref, x_ref, wb_ref, wo_ref, o_ref):
    merged = (g0_ref[...].astype(F32) * jnp.dot(od_ref[...], wb_ref[0], preferred_element_type=F32)
              + g1_ref[...].astype(F32) * jnp.dot(or_ref[...], wb_ref[1], preferred_element_type=F32)
              + g2_ref[...].astype(F32) * jnp.dot(ol_ref[...], wb_ref[2], preferred_element_type=F32))
    o_ref[...] = x_ref[...] + jnp.dot(merged.astype(BF16), wo_ref[...], preferred_element_type=F32)


def _merge(od, o_r, o_l, u2, x2, w_branch, w_out, layer, tm=256):
    m, d = x2.shape
    tm = min(tm, m)
    row = lambda: pl.BlockSpec((tm, d), lambda i: (i, 0))
    gate = lambda k: pl.BlockSpec((tm, d), lambda i: (i, COLB_GATES // d + k))
    return pl.pallas_call(
        _merge_kernel,
        grid=(m // tm,),
        in_specs=[
            row(), row(), row(), gate(0), gate(1), gate(2), row(),
            pl.BlockSpec((None, 3, d, d), lambda i: (layer, 0, 0, 0)),
            pl.BlockSpec((None, d, d), lambda i: (layer, 0, 0)),
        ],
        out_specs=row(),
        out_shape=jax.ShapeDtypeStruct((m, d), F32),
        compiler_params=_params("parallel"),
        name="merge",
    )(od, o_r, o_l, u2, u2, u2, x2, w_branch, w_out)


def _xattn_kernel(x_ref, g_ref, wq_ref, kv_ref, wo_ref, o_ref):
    x = x_ref[...]
    h = _rms_rows(x, g_ref[...], 1e-6).astype(BF16)
    q = jnp.dot(h, wq_ref[...], preferred_element_type=F32).astype(BF16)
    heads = []
    for hd in range(XA_HEADS):
        kc = slice(hd * XA_HEAD_DIM, (hd + 1) * XA_HEAD_DIM)
        vc = slice(XA_WIDTH + hd * XA_HEAD_DIM, XA_WIDTH + (hd + 1) * XA_HEAD_DIM)
        s = lax.dot_general(q[:, kc], kv_ref[:, kc], (((1,), (1,)), ((), ())),
                            preferred_element_type=F32) * (XA_HEAD_DIM ** -0.5)
        e = jnp.exp(s - jnp.max(s, axis=-1, keepdims=True))
        pv = jnp.dot(e.astype(BF16), kv_ref[:, vc], preferred_element_type=F32)
        heads.append((pv / jnp.sum(e, axis=-1, keepdims=True)).astype(BF16))
    o = jnp.concatenate(heads, axis=-1)
    o_ref[...] = x + jnp.dot(o, wo_ref[...], preferred_element_type=F32)


def _xattn(x3, g, wq, kv3, wo, layer, tm=512):
    bn, s_len, d = x3.shape
    tm = min(tm, s_len)
    m_len = kv3.shape[1]
    return pl.pallas_call(
        _xattn_kernel,
        grid=(bn, s_len // tm),
        in_specs=[
            pl.BlockSpec((None, tm, d), lambda b, i: (b, i, 0)),
            pl.BlockSpec((None, 1, d), lambda b, i: (layer, 0, 0)),
            pl.BlockSpec((None, d, XA_WIDTH), lambda b, i: (layer, 0, 0)),
            pl.BlockSpec((None, m_len, 2 * XA_WIDTH), lambda b, i: (b, 0, 0)),
            pl.BlockSpec((None, XA_WIDTH, d), lambda b, i: (layer, 0, 0)),
        ],
        out_specs=pl.BlockSpec((None, tm, d), lambda b, i: (b, i, 0)),
        out_shape=jax.ShapeDtypeStruct((bn, s_len, d), F32),
        compiler_params=_params("parallel", "parallel"),
        name="xattn",
    )(x3, g, wq, kv3, wo)


def _mlp_kernel(x_ref, g_ref, w1_ref, w2_ref, gf_ref, o_ref, *, final_norm):
    x = x_ref[...]
    h = _rms_rows(x, g_ref[...], 1e-6).astype(BF16)
    d = x.shape[1]
    acc = x
    for c in range(w1_ref.shape[1] // d):
        cols = slice(c * d, (c + 1) * d)
        a = jnp.maximum(jnp.dot(h, w1_ref[:, cols], preferred_element_type=F32), 0.0)
        acc = acc + jnp.dot((a * a).astype(BF16), w2_ref[cols, :], preferred_element_type=F32)
    o_ref[...] = _rms_rows(acc, gf_ref[...], 1e-6) if final_norm else acc


def _mlp(x2, g, w1, w2, g_final, layer, final_norm, tm=512):
    m, d = x2.shape
    tm = min(tm, m)
    dff = w1.shape[2]
    return pl.pallas_call(
        functools.partial(_mlp_kernel, final_norm=final_norm),
        grid=(m // tm,),
        in_specs=[
            pl.BlockSpec((tm, d), lambda i: (i, 0)),
            pl.BlockSpec((None, 1, d), lambda i: (layer, 0, 0)),
            pl.BlockSpec((None, d, dff), lambda i: (layer, 0, 0)),
            pl.BlockSpec((None, dff, d), lambda i: (layer, 0, 0)),
            pl.BlockSpec((1, d), lambda i: (0, 0)),
        ],
        out_specs=pl.BlockSpec((tm, d), lambda i: (i, 0)),
        out_shape=jax.ShapeDtypeStruct((m, d), F32),
        compiler_params=_params("parallel"),
        name="mlp",
    )(x2, g, w1, w2, g_final)


def kernel(x, mem, norm_mix, w_in, diff_lq1, diff_lk1, diff_lq2, diff_lk2, diff_subln, lru_conv_w, lru_conv_b, lru_wa, lru_ba, lru_wx, lru_bx, lru_lambda, w_branch, w_out, norm_xattn, norm_mem, xa_wq, xa_wkv, xa_wo, norm_mlp, mlp_w1, mlp_w2, norm_final):
    bn, s_len, d = x.shape
    m_len = mem.shape[1]
    depth = norm_mix.shape[0]
    rows = bn * s_len

    qk0, qk1 = 3 * PROJ_TN, 4 * PROJ_TN
    w_qk = w_in[:, :, qk0:qk1].reshape(depth, d, (qk1 - qk0) // RET_KEY_DIM, RET_KEY_DIM // 2, 2)
    w_qk = jnp.swapaxes(w_qk, -1, -2).reshape(depth, d, qk1 - qk0)
    w_dq = w_in[:, :, :PROJ_TN] * (DIFF_HEAD_DIM ** -0.5 * math.log2(math.e))
    w_in_b = jnp.concatenate([w_dq, w_in[:, :, PROJ_TN:qk0], w_qk, w_in[:, :, qk1:]], axis=-1).astype(BF16)
    w_branch_b, w_out_b = w_branch.astype(BF16), w_out.astype(BF16)
    wq_b, wkv_b, wo_b = xa_wq.astype(BF16), xa_wkv.astype(BF16), xa_wo.astype(BF16)
    w1_b, w2_b = mlp_w1.astype(BF16), mlp_w2.astype(BF16)
    wa_b, wx_b = lru_wa.astype(BF16), lru_wx.astype(BF16)
    row3 = lambda p: p.reshape(depth, 1, p.shape[-1])
    conv_w = lru_conv_w.reshape(depth, CONV_WIDTH, d)

    pos = jnp.arange(s_len, dtype=F32)
    angle = 1.0 / (10000.0 ** jnp.linspace(0.0, 1.0, RET_KEY_DIM // 2, dtype=F32))
    phase = pos[:, None] * angle[None, :]
    cos = jnp.concatenate([jnp.cos(phase), jnp.cos(phase)], axis=1)
    sin = jnp.concatenate([-jnp.sin(phase), jnp.sin(phase)], axis=1)
    log_g = jnp.log(1.0 - jnp.exp2(-5.0 - jnp.arange(RET_HEADS, dtype=F32)))
    lg = jnp.broadcast_to(log_g[:, None, None], (RET_HEADS, 1, RET_VAL_DIM))
    lam_inits = [0.8 - 0.6 * math.exp(-0.3 * l) for l in range(depth)]
    cst = jnp.zeros((depth, 1, 128), F32)
    cst = cst.at[:, 0, 0].set(jnp.asarray(lam_inits, F32))
    cst = cst.at[:, 0, 1].set(jnp.asarray([1.0 - v for v in lam_inits], F32))

    x2 = x.reshape(rows, d)
    mem2 = mem.reshape(bn * m_len, d)
    for l in range(depth):
        ub2, uf2 = _proj_in(x2, row3(norm_mix), w_in_b, l)
        ub3 = ub2.reshape(bn, s_len, ub2.shape[1])
        uf3 = uf2.reshape(bn, s_len, uf2.shape[1])
        od = _diff_attn(ub3, row3(diff_lq1), row3(diff_lk1), row3(diff_lq2), row3(diff_lk2), cst,
                        row3(diff_subln), l)
        o_r = _retention(ub3, uf3, lg, cos, sin)
        o_l = _rglru(ub3, uf3, conv_w, row3(lru_conv_b), wa_b, row3(lru_ba), wx_b, row3(lru_bx),
                     row3(lru_lambda), l)
        x2 = _merge(od.reshape(rows, d), o_r.reshape(rows, d), o_l.reshape(rows, d), ub2, x2,
                    w_branch_b, w_out_b, l)
        kv = _norm_matmul(mem2, row3(norm_mem), wkv_b, l, 2 * XA_WIDTH, BF16, tm=512, tn=2 * XA_WIDTH)
        x2 = _xattn(x2.reshape(bn, s_len, d), row3(norm_xattn), wq_b, kv.reshape(bn, m_len, 2 * XA_WIDTH),
                    wo_b, l).reshape(rows, d)
        x2 = _mlp(x2, row3(norm_mlp), w1_b, w2_b, norm_final.reshape(1, d), l, final_norm=(l == depth - 1))
    return x2.reshape(bn, s_len, d)
```

```python
import functools
import math

import jax
import jax.numpy as jnp
from jax import lax
from jax.experimental import pallas as pl
from jax.experimental.pallas import tpu as pltpu

F32 = jnp.float32
BF16 = jnp.bfloat16

D_MODEL = 1024
DIFF_HEADS = 8
DIFF_HEAD_DIM = 64
DIFF_VAL_DIM = 2 * DIFF_HEAD_DIM
RET_HEADS = 4
RET_KEY_DIM = 128
RET_VAL_DIM = 256
RET_CHUNK = 128
LRU_BLOCKS = 8
LRU_BLOCK_W = 128
CONV_WIDTH = 4
LRU_C = 8.0
XA_HEADS = 4
XA_HEAD_DIM = 128
XA_WIDTH = XA_HEADS * XA_HEAD_DIM
D_FF = 4 * D_MODEL

PROJ_TN = 1024
PROJ_CHUNK = 256
PROJ_TILES = (("dq", "qscale"), ("dk", "b"), ("dv", "b"), ("rqk", "f"), ("rv", "b"), ("rg", "silu"),
              ("lx", "f"), ("ly", "gelu"), ("g0", "sigmoid"), ("g1", "sigmoid"), ("g2", "sigmoid"))
PROJ_KINDS = tuple(kind for _, kind in PROJ_TILES)


def _proj_col(name):
    names = [n for n, _ in PROJ_TILES]
    kind = PROJ_KINDS[names.index(name)]
    same = [n for n, k in PROJ_TILES if (k == "f") == (kind == "f")]
    return same.index(name) * PROJ_TN


COLB_DQ, COLB_DK, COLB_DV, COLB_RV = (_proj_col(n) for n in ("dq", "dk", "dv", "rv"))
COLB_RG, COLB_LY, COLB_GATES = (_proj_col(n) for n in ("rg", "ly", "g0"))
COLF_RQ, COLF_LX = _proj_col("rqk"), _proj_col("lx")
COLF_RK = COLF_RQ + RET_HEADS * RET_KEY_DIM

VMEM_LIMIT_BYTES = 56 * 1024 * 1024

ATT_TQ = 256
ATT_TK = 256
ATT_HEADS_PER_STEP = 8
RET_CHUNKS_PER_TRIP = 4
RET_HEADS_PER_STEP = 2
LRU_TS = 512
SUBLANES = 8


def _params(*sem):
    return pltpu.CompilerParams(dimension_semantics=sem, vmem_limit_bytes=VMEM_LIMIT_BYTES)


def _rms_rows(xf, g, eps):
    return xf * lax.rsqrt(jnp.mean(xf * xf, axis=-1, keepdims=True) + eps) * g


def _norm_matmul_kernel(x_ref, g_ref, w_ref, o_ref, h_ref):
    @pl.when(pl.program_id(1) == 0)
    def _():
        h_ref[...] = _rms_rows(x_ref[...], g_ref[...], 1e-6).astype(BF16)

    o_ref[...] = jnp.dot(h_ref[...], w_ref[...], preferred_element_type=F32).astype(o_ref.dtype)


def _norm_matmul(x2, g, w, layer, n_cols, out_dtype, tm, tn):
    m, d = x2.shape
    tm = min(tm, m)
    return pl.pallas_call(
        _norm_matmul_kernel,
        grid=(m // tm, n_cols // tn),
        in_specs=[
            pl.BlockSpec((tm, d), lambda i, j: (i, 0)),
            pl.BlockSpec((None, 1, d), lambda i, j: (layer, 0, 0)),
            pl.BlockSpec((None, d, tn), lambda i, j: (layer, 0, j)),
        ],
        out_specs=pl.BlockSpec((tm, tn), lambda i, j: (i, j)),
        out_shape=jax.ShapeDtypeStruct((m, n_cols), out_dtype),
        scratch_shapes=[pltpu.VMEM((tm, d), BF16)],
        compiler_params=_params("parallel", "arbitrary"),
        name="norm_matmul",
    )(x2, g, w)


def _held_index(is_f32):
    table, n = [], 0
    for kind in PROJ_KINDS:
        if (kind == "f") == is_f32:
            n += 1
        table.append(max(n - 1, 0))
    steps = [j for j in range(1, len(table)) if table[j] != table[j - 1]]
    return lambda j: sum((j >= s).astype(jnp.int32) for s in steps)


def _gelu_tanh(y):
    c = math.sqrt(2.0 / math.pi)
    half = 0.5 * y
    return half + half * jnp.tanh(y * (c + (c * 0.044715) * (y * y)))


def _sigmoid(z):
    return 0.5 * jnp.tanh(0.5 * z) + 0.5


def _silu(z):
    half = 0.5 * z
    return half + half * jnp.tanh(half)


_PROJ_EPILOGUES = {
    "f": lambda u: u,
    "b": lambda u: u,
    "qscale": lambda u: u * (DIFF_HEAD_DIM ** -0.5 * math.log2(math.e)),
    "silu": _silu,
    "gelu": _gelu_tanh,
    "sigmoid": _sigmoid,
}


def _proj_in_kernel(x_ref, g_ref, w_ref, ob_ref, of_ref, h_ref):
    j = pl.program_id(1)

    @pl.when(j == 0)
    def _():
        h_ref[...] = _rms_rows(x_ref[...], g_ref[...], 1e-6).astype(BF16)

    for kind in sorted(set(PROJ_KINDS)):
        is_kind = functools.reduce(jnp.logical_or, [j == t for t, k in enumerate(PROJ_KINDS) if k == kind])
        out_ref = of_ref if kind == "f" else ob_ref

        @pl.when(is_kind)
        def _(kind=kind, out_ref=out_ref):
            for r in range(0, h_ref.shape[0], PROJ_CHUNK):
                u = jnp.dot(h_ref[r:r + PROJ_CHUNK, :], w_ref[...], preferred_element_type=F32)
                out_ref[r:r + PROJ_CHUNK, :] = _PROJ_EPILOGUES[kind](u).astype(out_ref.dtype)


def _proj_in(x2, g, w, layer, tm=2048):
    m, d = x2.shape
    tm = min(tm, m)
    tn = PROJ_TN
    idx_b, idx_f = _held_index(False), _held_index(True)
    n_f32 = PROJ_KINDS.count("f")
    return pl.pallas_call(
        _proj_in_kernel,
        grid=(m // tm, len(PROJ_KINDS)),
        in_specs=[
            pl.BlockSpec((tm, d), lambda i, j: (i, 0)),
            pl.BlockSpec((None, 1, d), lambda i, j: (layer, 0, 0)),
            pl.BlockSpec((None, d, tn), lambda i, j: (layer, 0, j)),
        ],
        out_specs=[
            pl.BlockSpec((tm, tn), lambda i, j: (i, idx_b(j))),
            pl.BlockSpec((tm, tn), lambda i, j: (i, idx_f(j))),
        ],
        out_shape=[
            jax.ShapeDtypeStruct((m, tn * (len(PROJ_KINDS) - n_f32)), BF16),
            jax.ShapeDtypeStruct((m, tn * n_f32), F32),
        ],
        scratch_shapes=[pltpu.VMEM((tm, d), BF16)],
        compiler_params=_params("parallel", "arbitrary"),
        name="proj_in",
    )(x2, g, w)


def _diff_attn_kernel(lq1_ref, lk1_ref, lq2_ref, lk2_ref, cst_ref, subln_ref, q_ref, k_ref, v_ref,
                      o_ref, qq_ref, vt_ref, s_ref, m_ref, l_ref, acc_ref):
    i = pl.program_id(2)
    tq, tk, wv = ATT_TQ, ATT_TK, DIFF_VAL_DIM
    n_kv = k_ref.shape[0] // tk
    heads = range(ATT_HEADS_PER_STEP)

    @pl.when(i == 0)
    def _():
        for hd in heads:
            for t in range(n_kv):
                v_t = v_ref[t * tk:(t + 1) * tk, hd * wv:(hd + 1) * wv]
                vt_ref[hd, t] = v_t.astype(F32).T.astype(BF16)

    lane = lax.broadcasted_iota(jnp.int32, (1, wv), 1)
    first_map = jnp.where(lane < DIFF_HEAD_DIM, 1.0, 0.0).astype(BF16)
    second_map = jnp.where(lane >= DIFF_HEAD_DIM, 1.0, 0.0).astype(BF16)
    for hd in heads:
        q = q_ref[:, hd * wv:(hd + 1) * wv]
        qq_ref[hd, 0:tq, :] = q * first_map
        qq_ref[hd, tq:2 * tq, :] = q * second_map

    m_ref[...] = jnp.full(m_ref.shape, -jnp.inf, F32)
    l_ref[...] = jnp.zeros(l_ref.shape, F32)
    acc_ref[...] = jnp.zeros(acc_ref.shape, F32)

    def scores(j, hd):
        k_t = k_ref[pl.ds(pl.multiple_of(j * tk, tk), tk), hd * wv:(hd + 1) * wv]
        s_ref[hd] = lax.dot_general(k_t, qq_ref[hd], (((1,), (1,)), ((), ())),
                                    preferred_element_type=F32)

    def accumulate(j, hd, masked):
        s = s_ref[hd]
        if masked:
            key = lax.broadcasted_iota(jnp.int32, (tk, tq), 0)
            qry = lax.broadcasted_iota(jnp.int32, (tk, tq), 1)
            ok = key <= qry
            s = jnp.where(jnp.concatenate([ok, ok], axis=1), s, -jnp.inf)
        m_old = m_ref[hd]
        m_new = jnp.maximum(m_old, jnp.max(s, axis=0, keepdims=True))
        alpha = jnp.exp2(m_old - m_new)
        e = jnp.exp2(s - m_new)
        l_ref[hd] = alpha * l_ref[hd] + jnp.sum(e, axis=0, keepdims=True)
        acc_ref[hd] = alpha * acc_ref[hd] + jnp.dot(vt_ref[hd, j], e.astype(BF16), preferred_element_type=F32)
        m_ref[hd] = m_new

    for hd in heads:
        scores(0, hd)

    def body(j, carry):
        for hd in heads:
            accumulate(j, hd, False)
            scores(j + 1, hd)
        return carry

    lax.fori_loop(0, i, body, 0)
    for hd in heads:
        accumulate(i, hd, True)

    lam_init = cst_ref[0:1, 0:1]
    out_scale = cst_ref[0:1, 1:2]
    lam = (jnp.exp(jnp.sum(lq1_ref[...] * lk1_ref[...], axis=-1, keepdims=True))
           - jnp.exp(jnp.sum(lq2_ref[...] * lk2_ref[...], axis=-1, keepdims=True)) + lam_init)
    for hd in heads:
        on = acc_ref[hd] * (1.0 / l_ref[hd])
        ot = on[:, :tq] - lam * on[:, tq:]
        inv = lax.rsqrt(jnp.mean(ot * ot, axis=0, keepdims=True) + 1e-5) * out_scale
        o_ref[:, hd * wv:(hd + 1) * wv] = (ot * inv * subln_ref[...]).T.astype(o_ref.dtype)


def _diff_attn(ub3, lq1, lk1, lq2, lk2, cst, subln, layer):
    bn, s_len, _ = ub3.shape
    tq, hp = ATT_TQ, ATT_HEADS_PER_STEP
    assert s_len % tq == 0 and DIFF_HEADS % hp == 0
    nq = s_len // tq
    wv = DIFF_VAL_DIM
    wb = hp * wv
    vec = lambda width: pl.BlockSpec((None, 1, width), lambda b, h, i: (layer, 0, 0))
    return pl.pallas_call(
        _diff_attn_kernel,
        grid=(bn, DIFF_HEADS // hp, nq),
        in_specs=[
            vec(DIFF_HEAD_DIM), vec(DIFF_HEAD_DIM), vec(DIFF_HEAD_DIM), vec(DIFF_HEAD_DIM),
            vec(128),
            pl.BlockSpec((None, wv, tq), lambda b, h, i: (layer, 0, 0)),
            pl.BlockSpec((None, tq, wb), lambda b, h, i: (b, i, COLB_DQ // wb + h)),
            pl.BlockSpec((None, s_len, wb), lambda b, h, i: (b, 0, COLB_DK // wb + h)),
            pl.BlockSpec((None, s_len, wb), lambda b, h, i: (b, 0, COLB_DV // wb + h)),
        ],
        out_specs=pl.BlockSpec((None, tq, wb), lambda b, h, i: (b, i, h)),
        out_shape=jax.ShapeDtypeStruct((bn, s_len, DIFF_HEADS * wv), BF16),
        scratch_shapes=[
            pltpu.VMEM((hp, 2 * tq, wv), BF16),
            pltpu.VMEM((hp, s_len // ATT_TK, wv, ATT_TK), BF16),
            pltpu.VMEM((hp, ATT_TK, 2 * tq), F32),
            pltpu.VMEM((hp, 1, 2 * tq), F32),
            pltpu.VMEM((hp, 1, 2 * tq), F32),
            pltpu.VMEM((hp, wv, 2 * tq), F32),
        ],
        compiler_params=_params("parallel", "parallel", "arbitrary"),
        name="diff_attn",
    )(lq1, lk1, lq2, lk2, cst, subln, ub3, ub3, ub3)


def _retention_kernel(lg_ref, cos_ref, sin_ref, q_ref, k_ref, v_ref, g_ref, o_ref, state_ref):
    c = RET_CHUNK
    dk, dv = RET_KEY_DIM, RET_VAL_DIM
    n_chunks = q_ref.shape[0] // c
    heads = range(RET_HEADS_PER_STEP)
    row = lax.broadcasted_iota(jnp.int32, (c, c), 0).astype(F32)
    col = lax.broadcasted_iota(jnp.int32, (c, c), 1).astype(F32)
    rel = row - col
    row_v = lax.broadcasted_iota(jnp.int32, (c, dv), 0).astype(F32)
    tables = []
    for hd in heads:
        lg = lg_ref[hd]
        decay = jnp.where(rel >= 0, jnp.exp(jnp.maximum(rel, 0.0) * lg[:, :c]), 0.0)
        w_k = jnp.exp((c - 1.0 - row) * lg[:, :c])
        w_q = jnp.exp((row_v + 1.0) * lg)
        chunk_decay = jnp.exp(float(c) * lg)
        tables.append((decay, w_k, w_q, chunk_decay))

    def rotary(t, cos, sin):
        return t * cos + pltpu.roll(t, dk // 2, axis=1) * sin

    state_ref[...] = jnp.zeros(state_ref.shape, F32)

    def body(nn, carry):
        work = {}
        for t in range(RET_CHUNKS_PER_TRIP):
            for hd in heads:
                _, w_k, _, _ = tables[hd]
                rows = pl.ds(pl.multiple_of((nn * RET_CHUNKS_PER_TRIP + t) * c, c), c)
                kcols = slice(hd * dk, (hd + 1) * dk)
                vcols = slice(hd * dv, (hd + 1) * dv)
                cos, sin = cos_ref[rows, :], sin_ref[rows, :]
                qb = rotary(q_ref[rows, kcols], cos, sin).astype(BF16)
                kr = rotary(k_ref[rows, kcols], cos, sin) * (dk ** -0.5)
                vb = v_ref[rows, vcols]
                s = lax.dot_general(qb, kr.astype(BF16), (((1,), (1,)), ((), ())), preferred_element_type=F32)
                kv = jnp.dot((kr * w_k).T.astype(BF16), vb, preferred_element_type=F32)
                work[t, hd] = (rows, vcols, qb, vb, s, kv)
        outs = {}
        for hd in heads:
            decay, _, w_q, chunk_decay = tables[hd]
            state = state_ref[hd]
            for t in range(RET_CHUNKS_PER_TRIP):
                rows, vcols, qb, vb, s, kv = work[t, hd]
                outs[t, hd] = (jnp.dot((s * decay).astype(BF16), vb, preferred_element_type=F32)
                               + jnp.dot(qb, state.astype(BF16), preferred_element_type=F32) * w_q)
                state = chunk_decay * state + kv
            state_ref[hd] = state
        for (t, hd), o in outs.items():
            rows, vcols = work[t, hd][:2]
            mu = jnp.mean(o, axis=-1, keepdims=True)
            var = jnp.mean(jnp.square(o - mu), axis=-1, keepdims=True)
            on = (o - mu) * lax.rsqrt(var + 1e-5)
            o_ref[rows, vcols] = (g_ref[rows, vcols].astype(F32) * on).astype(o_ref.dtype)
        return carry

    lax.fori_loop(0, n_chunks // RET_CHUNKS_PER_TRIP, body, 0)


def _retention(ub3, uf3, lg, cos, sin):
    bn, s_len, _ = uf3.shape
    dk, dv = RET_KEY_DIM, RET_VAL_DIM
    hp = RET_HEADS_PER_STEP
    assert s_len % (RET_CHUNK * RET_CHUNKS_PER_TRIP) == 0 and RET_HEADS % hp == 0
    wk, wv = hp * dk, hp * dv
    return pl.pallas_call(
        _retention_kernel,
        grid=(bn, RET_HEADS // hp),
        in_specs=[
            pl.BlockSpec((hp, 1, dv), lambda b, h: (h, 0, 0)),
            pl.BlockSpec((s_len, dk), lambda b, h: (0, 0)),
            pl.BlockSpec((s_len, dk), lambda b, h: (0, 0)),
            pl.BlockSpec((None, s_len, wk), lambda b, h: (b, 0, COLF_RQ // wk + h)),
            pl.BlockSpec((None, s_len, wk), lambda b, h: (b, 0, COLF_RK // wk + h)),
            pl.BlockSpec((None, s_len, wv), lambda b, h: (b, 0, COLB_RV // wv + h)),
            pl.BlockSpec((None, s_len, wv), lambda b, h: (b, 0, COLB_RG // wv + h)),
        ],
        out_specs=pl.BlockSpec((None, s_len, wv), lambda b, h: (b, 0, h)),
        out_shape=jax.ShapeDtypeStruct((bn, s_len, RET_HEADS * dv), BF16),
        scratch_shapes=[pltpu.VMEM((hp, dk, dv), F32)],
        compiler_params=_params("parallel", "parallel"),
        name="retention",
    )(lg, cos, sin, uf3, uf3, ub3, ub3)


def _softplus(z):
    return jnp.maximum(z, 0.0) + jnp.log1p(jnp.exp(-jnp.abs(z)))


def _rglru_kernel(cw_ref, cb_ref, wa_ref, ba_ref, wx_ref, bx_ref, lam_ref, x_ref, y_ref, o_ref,
                  xpad_ref, h_ref):
    ts = x_ref.shape[0]
    halo = SUBLANES
    groups = ts // SUBLANES

    @pl.when(pl.program_id(1) == 0)
    def _():
        xpad_ref[0:halo, :] = jnp.zeros((halo, xpad_ref.shape[1]), F32)
        h_ref[...] = jnp.zeros(h_ref.shape, F32)

    @pl.when(pl.program_id(1) > 0)
    def _():
        xpad_ref[0:halo, :] = xpad_ref[ts:ts + halo, :]

    xpad_ref[halo:halo + ts, :] = x_ref[...]
    sub = lax.broadcasted_iota(jnp.int32, (groups, SUBLANES, LRU_BLOCK_W), 1)

    for n in range(LRU_BLOCKS):
        cols = slice(n * LRU_BLOCK_W, (n + 1) * LRU_BLOCK_W)
        xc = cb_ref[:, cols]
        for j in range(CONV_WIDTH):
            off = halo + j - (CONV_WIDTH - 1)
            xc = xc + cw_ref[j:j + 1, cols] * xpad_ref[off:off + ts, cols]
        xb = xc.astype(BF16)
        r = _sigmoid(jnp.dot(xb, wa_ref[n], preferred_element_type=F32) + ba_ref[:, cols])
        gi = _sigmoid(jnp.dot(xb, wx_ref[n], preferred_element_type=F32) + bx_ref[:, cols])
        log_a = (-LRU_C * _softplus(-lam_ref[:, cols])) * r
        a = jnp.exp(log_a)
        b = jnp.sqrt(1.0 - a * a) * (gi * xc)

        a3 = a.reshape(groups, SUBLANES, LRU_BLOCK_W)
        b3 = b.reshape(groups, SUBLANES, LRU_BLOCK_W)
        d = 1
        while d < SUBLANES:
            keep = sub >= d
            b3 = jnp.where(keep, a3 * pltpu.roll(b3, d, axis=1) + b3, b3)
            a3 = jnp.where(keep, a3 * pltpu.roll(a3, d, axis=1), a3)
            d *= 2
        h = h_ref[:, cols]
        hs = []
        for g in range(groups):
            hg = b3[g] + a3[g] * h
            h = hg[SUBLANES - 1:SUBLANES, :]
            hs.append(hg)
        h_ref[:, cols] = h
        o_ref[:, cols] = (jnp.concatenate(hs, axis=0) * y_ref[:, cols].astype(F32)).astype(o_ref.dtype)


def _rglru(ub3, uf3, conv_w, conv_b, wa, ba, wx, bx, lam, layer):
    bn, s_len, _ = uf3.shape
    w = LRU_BLOCKS * LRU_BLOCK_W
    ts = min(LRU_TS, s_len)
    vec = lambda: pl.BlockSpec((None, 1, w), lambda b, t: (layer, 0, 0))
    blk = lambda: pl.BlockSpec((None, LRU_BLOCKS, LRU_BLOCK_W, LRU_BLOCK_W), lambda b, t: (layer, 0, 0, 0))
    return pl.pallas_call(
        _rglru_kernel,
        grid=(bn, s_len // ts),
        in_specs=[
            pl.BlockSpec((None, CONV_WIDTH, w), lambda b, t: (layer, 0, 0)),
            vec(), blk(), vec(), blk(), vec(), vec(),
            pl.BlockSpec((None, ts, w), lambda b, t: (b, t, COLF_LX // w)),
            pl.BlockSpec((None, ts, w), lambda b, t: (b, t, COLB_LY // w)),
        ],
        out_specs=pl.BlockSpec((None, ts, w), lambda b, t: (b, t, 0)),
        out_shape=jax.ShapeDtypeStruct((bn, s_len, w), BF16),
        scratch_shapes=[pltpu.VMEM((ts + SUBLANES, w), F32), pltpu.VMEM((1, w), F32)],
        compiler_params=_params("parallel", "arbitrary"),
        name="rglru",
    )(conv_w, conv_b, wa, ba, wx, bx, lam, uf3, ub3)


def _merge_kernel(od_ref, or_ref, ol_ref, g0_ref, g1_ref, g2_ref, x_ref, wb_ref, wo_ref, o_ref):
    merged = (g0_ref[...].astype(F32) * jnp.dot(od_ref[...], wb_ref[0], preferred_element_type=F32)
              + g1_ref[...].astype(F32) * jnp.dot(or_ref[...], wb_ref[1], preferred_element_type=F32)
              + g2_ref[...].astype(F32) * jnp.dot(ol_ref[...], wb_ref[2], preferred_element_type=F32))
    o_ref[...] = x_ref[...] + jnp.dot(merged.astype(BF16), wo_ref[...], preferred_element_type=F32)


def _merge(od, o_r, o_l, u2, x2, w_branch, w_out, layer, tm=512):
    m, d = x2.shape
    tm = min(tm, m)
    row = lambda: pl.BlockSpec((tm, d), lambda i: (i, 0))
    gate = lambda k: pl.BlockSpec((tm, d), lambda i: (i, COLB_GATES // d + k))
    return pl.pallas_call(
        _merge_kernel,
        grid=(m // tm,),
        in_specs=[
            row(), row(), row(), gate(0), gate(1), gate(2), row(),
            pl.BlockSpec((None, 3, d, d), lambda i: (layer, 0, 0, 0)),
            pl.BlockSpec((None, d, d), lambda i: (layer, 0, 0)),
        ],
        out_specs=row(),
        out_shape=jax.ShapeDtypeStruct((m, d), F32),
        compiler_params=_params("parallel"),
        name="merge",
    )(od, o_r, o_l, u2, u2, u2, x2, w_branch, w_out)


def _xattn_kernel(x_ref, g_ref, wq_ref, kv_ref, wo_ref, o_ref):
    x = x_ref[...]
    h = _rms_rows(x, g_ref[...], 1e-6).astype(BF16)
    q = jnp.dot(h, wq_ref[...], preferred_element_type=F32).astype(BF16)
    heads = []
    for hd in range(XA_HEADS):
        kc = slice(hd * XA_HEAD_DIM, (hd + 1) * XA_HEAD_DIM)
        vc = slice(XA_WIDTH + hd * XA_HEAD_DIM, XA_WIDTH + (hd + 1) * XA_HEAD_DIM)
        s = lax.dot_general(q[:, kc], kv_ref[:, kc], (((1,), (1,)), ((), ())),
                            preferred_element_type=F32) * (XA_HEAD_DIM ** -0.5)
        e = jnp.exp(s - jnp.max(s, axis=-1, keepdims=True))
        pv = jnp.dot(e.astype(BF16), kv_ref[:, vc], preferred_element_type=F32)
        heads.append((pv / jnp.sum(e, axis=-1, keepdims=True)).astype(BF16))
    o = jnp.concatenate(heads, axis=-1)
    o_ref[...] = x + jnp.dot(o, wo_ref[...], preferred_element_type=F32)


def _xattn(x3, g, wq, kv3, wo, layer, tm=512):
    bn, s_len, d = x3.shape
    tm = min(tm, s_len)
    m_len = kv3.shape[1]
    return pl.pallas_call(
        _xattn_kernel,
        grid=(bn, s_len // tm),
        in_specs=[
            pl.BlockSpec((None, tm, d), lambda b, i: (b, i, 0)),
            pl.BlockSpec((None, 1, d), lambda b, i: (layer, 0, 0)),
            pl.BlockSpec((None, d, XA_WIDTH), lambda b, i: (layer, 0, 0)),
            pl.BlockSpec((None, m_len, 2 * XA_WIDTH), lambda b, i: (b, 0, 0)),
            pl.BlockSpec((None, XA_WIDTH, d), lambda b, i: (layer, 0, 0)),
        ],
        out_specs=pl.BlockSpec((None, tm, d), lambda b, i: (b, i, 0)),
        out_shape=jax.ShapeDtypeStruct((bn, s_len, d), F32),
        compiler_params=_params("parallel", "parallel"),
        name="xattn",
    )(x3, g, wq, kv3, wo)


def _mlp_kernel(x_ref, g_ref, w1_ref, w2_ref, gf_ref, o_ref, *, final_norm):
    x = x_ref[...]
    h = _rms_rows(x, g_ref[...], 1e-6).astype(BF16)
    d = x.shape[1]
    acc = x
    for c in range(w1_ref.shape[1] // d):
        cols = slice(c * d, (c + 1) * d)
        a = jnp.maximum(jnp.dot(h, w1_ref[:, cols], preferred_element_type=F32), 0.0)
        acc = acc + jnp.dot((a * a).astype(BF16), w2_ref[cols, :], preferred_element_type=F32)
    o_ref[...] = _rms_rows(acc, gf_ref[...], 1e-6) if final_norm else acc


def _mlp(x2, g, w1, w2, g_final, layer, final_norm, tm=512):
    m, d = x2.shape
    tm = min(tm, m)
    dff = w1.shape[2]
    return pl.pallas_call(
        functools.partial(_mlp_kernel, final_norm=final_norm),
        grid=(m // tm,),
        in_specs=[
            pl.BlockSpec((tm, d), lambda i: (i, 0)),
            pl.BlockSpec((None, 1, d), lambda i: (layer, 0, 0)),
            pl.BlockSpec((None, d, dff), lambda i: (layer, 0, 0)),
            pl.BlockSpec((None, dff, d), lambda i: (layer, 0, 0)),
            pl.BlockSpec((1, d), lambda i: (0, 0)),
        ],
        out_specs=pl.BlockSpec((tm, d), lambda i: (i, 0)),
        out_shape=jax.ShapeDtypeStruct((m, d), F32),
        compiler_params=_params("parallel"),
        name="mlp",
    )(x2, g, w1, w2, g_final)


def kernel(x, mem, norm_mix, w_in, diff_lq1, diff_lk1, diff_lq2, diff_lk2, diff_subln, lru_conv_w, lru_conv_b, lru_wa, lru_ba, lru_wx, lru_bx, lru_lambda, w_branch, w_out, norm_xattn, norm_mem, xa_wq, xa_wkv, xa_wo, norm_mlp, mlp_w1, mlp_w2, norm_final):
    bn, s_len, d = x.shape
    m_len = mem.shape[1]
    depth = norm_mix.shape[0]
    rows = bn * s_len

    qk0, qk1 = 3 * PROJ_TN, 4 * PROJ_TN
    w_qk = w_in[:, :, qk0:qk1].reshape(depth, d, (qk1 - qk0) // RET_KEY_DIM, RET_KEY_DIM // 2, 2)
    w_qk = jnp.swapaxes(w_qk, -1, -2).reshape(depth, d, qk1 - qk0)
    w_in_b = lax.dynamic_update_slice(w_in.astype(BF16), w_qk.astype(BF16), (0, 0, qk0))
    w_branch_b, w_out_b = w_branch.astype(BF16), w_out.astype(BF16)
    wq_b, wkv_b, wo_b = xa_wq.astype(BF16), xa_wkv.astype(BF16), xa_wo.astype(BF16)
    w1_b, w2_b = mlp_w1.astype(BF16), mlp_w2.astype(BF16)
    wa_b, wx_b = lru_wa.astype(BF16), lru_wx.astype(BF16)
    row3 = lambda p: p.reshape(depth, 1, p.shape[-1])
    subln_lanes = jnp.broadcast_to(diff_subln[:, :, None], (depth, DIFF_VAL_DIM, ATT_TQ))
    conv_w = lru_conv_w.reshape(depth, CONV_WIDTH, d)

    pos = jnp.arange(s_len, dtype=F32)
    angle = 1.0 / (10000.0 ** jnp.linspace(0.0, 1.0, RET_KEY_DIM // 2, dtype=F32))
    phase = pos[:, None] * angle[None, :]
    cos = jnp.concatenate([jnp.cos(phase), jnp.cos(phase)], axis=1)
    sin = jnp.concatenate([-jnp.sin(phase), jnp.sin(phase)], axis=1)
    log_g = jnp.log(1.0 - jnp.exp2(-5.0 - jnp.arange(RET_HEADS, dtype=F32)))
    lg = jnp.broadcast_to(log_g[:, None, None], (RET_HEADS, 1, RET_VAL_DIM))
    lam_inits = [0.8 - 0.6 * math.exp(-0.3 * l) for l in range(depth)]
    cst = jnp.zeros((depth, 1, 128), F32)
    cst = cst.at[:, 0, 0].set(jnp.asarray(lam_inits, F32))
    cst = cst.at[:, 0, 1].set(jnp.asarray([1.0 - v for v in lam_inits], F32))

    x2 = x.reshape(rows, d)
    mem2 = mem.reshape(bn * m_len, d)
    for l in range(depth):
        ub2, uf2 = _proj_in(x2, row3(norm_mix), w_in_b, l)
        ub3 = ub2.reshape(bn, s_len, ub2.shape[1])
        uf3 = uf2.reshape(bn, s_len, uf2.shape[1])
        od = _diff_attn(ub3, row3(diff_lq1), row3(diff_lk1), row3(diff_lq2), row3(diff_lk2), cst,
                        subln_lanes, l)
        o_r = _retention(ub3, uf3, lg, cos, sin)
        o_l = _rglru(ub3, uf3, conv_w, row3(lru_conv_b), wa_b, row3(lru_ba), wx_b, row3(lru_bx),
                     row3(lru_lambda), l)
        x2 = _merge(od.reshape(rows, d), o_r.reshape(rows, d), o_l.reshape(rows, d), ub2, x2,
                    w_branch_b, w_out_b, l)
        kv = _norm_matmul(mem2, row3(norm_mem), wkv_b, l, 2 * XA_WIDTH, BF16, tm=512, tn=2 * XA_WIDTH)
        x2 = _xattn(x2.reshape(bn, s_len, d), row3(norm_xattn), wq_b, kv.reshape(bn, m_len, 2 * XA_WIDTH),
                    wo_b, l).reshape(rows, d)
        x2 = _mlp(x2, row3(norm_mlp), w1_b, w2_b, norm_final.reshape(1, d), l, final_norm=(l == depth - 1))
    return x2.reshape(bn, s_len, d)
```

```python
import functools
import math

import jax
import jax.numpy as jnp
from jax import lax
from jax.experimental import pallas as pl
from jax.experimental.pallas import tpu as pltpu

F32 = jnp.float32
BF16 = jnp.bfloat16

D_MODEL = 1024
DIFF_HEADS = 8
DIFF_HEAD_DIM = 64
DIFF_VAL_DIM = 2 * DIFF_HEAD_DIM
RET_HEADS = 4
RET_KEY_DIM = 128
RET_VAL_DIM = 256
RET_CHUNK = 128
LRU_BLOCKS = 8
LRU_BLOCK_W = 128
CONV_WIDTH = 4
LRU_C = 8.0
XA_HEADS = 4
XA_HEAD_DIM = 128
XA_WIDTH = XA_HEADS * XA_HEAD_DIM
D_FF = 4 * D_MODEL

PROJ_TN = 1024
PROJ_CHUNK = 256
PROJ_TILES = (("dq", "qscale"), ("dk", "b"), ("dv", "b"), ("rqk", "f"), ("rv", "b"), ("rg", "silu"),
              ("lx", "f"), ("ly", "gelu"), ("g0", "sigmoid"), ("g1", "sigmoid"), ("g2", "sigmoid"))
PROJ_KINDS = tuple(kind for _, kind in PROJ_TILES)


def _proj_col(name):
    names = [n for n, _ in PROJ_TILES]
    kind = PROJ_KINDS[names.index(name)]
    same = [n for n, k in PROJ_TILES if (k == "f") == (kind == "f")]
    return same.index(name) * PROJ_TN


COLB_DQ, COLB_DK, COLB_DV, COLB_RV = (_proj_col(n) for n in ("dq", "dk", "dv", "rv"))
COLB_RG, COLB_LY, COLB_GATES = (_proj_col(n) for n in ("rg", "ly", "g0"))
COLF_RQ, COLF_LX = _proj_col("rqk"), _proj_col("lx")
COLF_RK = COLF_RQ + RET_HEADS * RET_KEY_DIM

VMEM_LIMIT_BYTES = 56 * 1024 * 1024

ATT_TQ = 256
ATT_TK = 256
ATT_HEADS_PER_STEP = 8
RET_CHUNKS_PER_TRIP = 4
RET_HEADS_PER_STEP = 2
LRU_TS = 512
SUBLANES = 8


def _params(*sem):
    return pltpu.CompilerParams(dimension_semantics=sem, vmem_limit_bytes=VMEM_LIMIT_BYTES)


def _rms_rows(xf, g, eps):
    return xf * lax.rsqrt(jnp.mean(xf * xf, axis=-1, keepdims=True) + eps) * g


def _norm_matmul_kernel(x_ref, g_ref, w_ref, o_ref, h_ref):
    @pl.when(pl.program_id(1) == 0)
    def _():
        h_ref[...] = _rms_rows(x_ref[...], g_ref[...], 1e-6).astype(BF16)

    o_ref[...] = jnp.dot(h_ref[...], w_ref[...], preferred_element_type=F32).astype(o_ref.dtype)


def _norm_matmul(x2, g, w, layer, n_cols, out_dtype, tm, tn):
    m, d = x2.shape
    tm = min(tm, m)
    return pl.pallas_call(
        _norm_matmul_kernel,
        grid=(m // tm, n_cols // tn),
        in_specs=[
            pl.BlockSpec((tm, d), lambda i, j: (i, 0)),
            pl.BlockSpec((None, 1, d), lambda i, j: (layer, 0, 0)),
            pl.BlockSpec((None, d, tn), lambda i, j: (layer, 0, j)),
        ],
        out_specs=pl.BlockSpec((tm, tn), lambda i, j: (i, j)),
        out_shape=jax.ShapeDtypeStruct((m, n_cols), out_dtype),
        scratch_shapes=[pltpu.VMEM((tm, d), BF16)],
        compiler_params=_params("parallel", "arbitrary"),
        name="norm_matmul",
    )(x2, g, w)


def _held_index(is_f32):
    table, n = [], 0
    for kind in PROJ_KINDS:
        if (kind == "f") == is_f32:
            n += 1
        table.append(max(n - 1, 0))
    steps = [j for j in range(1, len(table)) if table[j] != table[j - 1]]
    return lambda j: sum((j >= s).astype(jnp.int32) for s in steps)


def _gelu_tanh(y):
    c = math.sqrt(2.0 / math.pi)
    half = 0.5 * y
    return half + half * jnp.tanh(y * (c + (c * 0.044715) * (y * y)))


def _sigmoid(z):
    return 0.5 * jnp.tanh(0.5 * z) + 0.5


def _silu(z):
    half = 0.5 * z
    return half + half * jnp.tanh(half)


_PROJ_EPILOGUES = {
    "f": lambda u: u,
    "b": lambda u: u,
    "qscale": lambda u: u * (DIFF_HEAD_DIM ** -0.5 * math.log2(math.e)),
    "silu": _silu,
    "gelu": _gelu_tanh,
    "sigmoid": _sigmoid,
}


def _proj_in_kernel(x_ref, g_ref, w_ref, ob_ref, of_ref, h_ref):
    j = pl.program_id(1)

    @pl.when(j == 0)
    def _():
        h_ref[...] = _rms_rows(x_ref[...], g_ref[...], 1e-6).astype(BF16)

    for kind in sorted(set(PROJ_KINDS)):
        is_kind = functools.reduce(jnp.logical_or, [j == t for t, k in enumerate(PROJ_KINDS) if k == kind])
        out_ref = of_ref if kind == "f" else ob_ref

        @pl.when(is_kind)
        def _(kind=kind, out_ref=out_ref):
            for r in range(0, h_ref.shape[0], PROJ_CHUNK):
                u = jnp.dot(h_ref[r:r + PROJ_CHUNK, :], w_ref[...], preferred_element_type=F32)
                out_ref[r:r + PROJ_CHUNK, :] = _PROJ_EPILOGUES[kind](u).astype(out_ref.dtype)


def _proj_in(x2, g, w, layer, tm=2048):
    m, d = x2.shape
    tm = min(tm, m)
    tn = PROJ_TN
    idx_b, idx_f = _held_index(False), _held_index(True)
    n_f32 = PROJ_KINDS.count("f")
    return pl.pallas_call(
        _proj_in_kernel,
        grid=(m // tm, len(PROJ_KINDS)),
        in_specs=[
            pl.BlockSpec((tm, d), lambda i, j: (i, 0)),
            pl.BlockSpec((None, 1, d), lambda i, j: (layer, 0, 0)),
            pl.BlockSpec((None, d, tn), lambda i, j: (layer, 0, j)),
        ],
        out_specs=[
            pl.BlockSpec((tm, tn), lambda i, j: (i, idx_b(j))),
            pl.BlockSpec((tm, tn), lambda i, j: (i, idx_f(j))),
        ],
        out_shape=[
            jax.ShapeDtypeStruct((m, tn * (len(PROJ_KINDS) - n_f32)), BF16),
            jax.ShapeDtypeStruct((m, tn * n_f32), F32),
        ],
        scratch_shapes=[pltpu.VMEM((tm, d), BF16)],
        compiler_params=_params("parallel", "arbitrary"),
        name="proj_in",
    )(x2, g, w)


def _diff_attn_kernel(lq1_ref, lk1_ref, lq2_ref, lk2_ref, cst_ref, subln_ref, q_ref, k_ref, v_ref,
                      o_ref, qq_ref, vt_ref, s_ref, m_ref, l_ref, acc_ref):
    i = pl.program_id(2)
    tq, tk, wv = ATT_TQ, ATT_TK, DIFF_VAL_DIM
    n_kv = k_ref.shape[0] // tk
    heads = range(ATT_HEADS_PER_STEP)

    @pl.when(i == 0)
    def _():
        for hd in heads:
            for t in range(n_kv):
                v_t = v_ref[t * tk:(t + 1) * tk, hd * wv:(hd + 1) * wv]
                vt_ref[hd, t] = v_t.astype(F32).T.astype(BF16)

    lane = lax.broadcasted_iota(jnp.int32, (1, wv), 1)
    first_map = jnp.where(lane < DIFF_HEAD_DIM, 1.0, 0.0).astype(BF16)
    second_map = jnp.where(lane >= DIFF_HEAD_DIM, 1.0, 0.0).astype(BF16)
    for hd in heads:
        q = q_ref[:, hd * wv:(hd + 1) * wv]
        qq_ref[hd, 0:tq, :] = q * first_map
        qq_ref[hd, tq:2 * tq, :] = q * second_map

    m_ref[...] = jnp.full(m_ref.shape, -jnp.inf, F32)
    l_ref[...] = jnp.zeros(l_ref.shape, F32)
    acc_ref[...] = jnp.zeros(acc_ref.shape, F32)

    def scores(j, hd):
        k_t = k_ref[pl.ds(pl.multiple_of(j * tk, tk), tk), hd * wv:(hd + 1) * wv]
        s_ref[hd] = lax.dot_general(k_t, qq_ref[hd], (((1,), (1,)), ((), ())),
                                    preferred_element_type=F32)

    def accumulate(j, hd, masked):
        s = s_ref[hd]
        if masked:
            key = lax.broadcasted_iota(jnp.int32, (tk, tq), 0)
            qry = lax.broadcasted_iota(jnp.int32, (tk, tq), 1)
            ok = key <= qry
            s = jnp.where(jnp.concatenate([ok, ok], axis=1), s, -jnp.inf)
        m_old = m_ref[hd]
        m_new = jnp.maximum(m_old, jnp.max(s, axis=0, keepdims=True))
        alpha = jnp.exp2(m_old - m_new)
        e = jnp.exp2(s - m_new)
        l_ref[hd] = alpha * l_ref[hd] + jnp.sum(e, axis=0, keepdims=True)
        acc_ref[hd] = alpha * acc_ref[hd] + jnp.dot(vt_ref[hd, j], e.astype(BF16), preferred_element_type=F32)
        m_ref[hd] = m_new

    for hd in heads:
        scores(0, hd)

    def body(j, carry):
        for hd in heads:
            accumulate(j, hd, False)
            scores(j + 1, hd)
        return carry

    lax.fori_loop(0, i, body, 0)
    for hd in heads:
        accumulate(i, hd, True)

    lam_init = cst_ref[0:1, 0:1]
    out_scale = cst_ref[0:1, 1:2]
    lam = (jnp.exp(jnp.sum(lq1_ref[...] * lk1_ref[...], axis=-1, keepdims=True))
           - jnp.exp(jnp.sum(lq2_ref[...] * lk2_ref[...], axis=-1, keepdims=True)) + lam_init)
    for hd in heads:
        on = acc_ref[hd] * (1.0 / l_ref[hd])
        ot = on[:, :tq] - lam * on[:, tq:]
        inv = lax.rsqrt(jnp.mean(ot * ot, axis=0, keepdims=True) + 1e-5) * out_scale
        o_ref[:, hd * wv:(hd + 1) * wv] = (ot * inv * subln_ref[...]).T.astype(o_ref.dtype)


def _diff_attn(ub3, lq1, lk1, lq2, lk2, cst, subln, layer):
    bn, s_len, _ = ub3.shape
    tq, hp = ATT_TQ, ATT_HEADS_PER_STEP
    assert s_len % tq == 0 and DIFF_HEADS % hp == 0
    nq = s_len // tq
    wv = DIFF_VAL_DIM
    wb = hp * wv
    vec = lambda width: pl.BlockSpec((None, 1, width), lambda b, h, i: (layer, 0, 0))
    return pl.pallas_call(
        _diff_attn_kernel,
        grid=(bn, DIFF_HEADS // hp, nq),
        in_specs=[
            vec(DIFF_HEAD_DIM), vec(DIFF_HEAD_DIM), vec(DIFF_HEAD_DIM), vec(DIFF_HEAD_DIM),
            vec(128),
            pl.BlockSpec((None, wv, tq), lambda b, h, i: (layer, 0, 0)),
            pl.BlockSpec((None, tq, wb), lambda b, h, i: (b, i, COLB_DQ // wb + h)),
            pl.BlockSpec((None, s_len, wb), lambda b, h, i: (b, 0, COLB_DK // wb + h)),
            pl.BlockSpec((None, s_len, wb), lambda b, h, i: (b, 0, COLB_DV // wb + h)),
        ],
        out_specs=pl.BlockSpec((None, tq, wb), lambda b, h, i: (b, i, h)),
        out_shape=jax.ShapeDtypeStruct((bn, s_len, DIFF_HEADS * wv), BF16),
        scratch_shapes=[
            pltpu.VMEM((hp, 2 * tq, wv), BF16),
            pltpu.VMEM((hp, s_len // ATT_TK, wv, ATT_TK), BF16),
            pltpu.VMEM((hp, ATT_TK, 2 * tq), F32),
            pltpu.VMEM((hp, 1, 2 * tq), F32),
            pltpu.VMEM((hp, 1, 2 * tq), F32),
            pltpu.VMEM((hp, wv, 2 * tq), F32),
        ],
        compiler_params=_params("parallel", "parallel", "arbitrary"),
        name="diff_attn",
    )(lq1, lk1, lq2, lk2, cst, subln, ub3, ub3, ub3)


def _retention_kernel(lg_ref, cos_ref, sin_ref, q_ref, k_ref, v_ref, g_ref, o_ref, state_ref):
    c = RET_CHUNK
    dk, dv = RET_KEY_DIM, RET_VAL_DIM
    n_chunks = q_ref.shape[0] // c
    heads = range(RET_HEADS_PER_STEP)
    row = lax.broadcasted_iota(jnp.int32, (c, c), 0).astype(F32)
    col = lax.broadcasted_iota(jnp.int32, (c, c), 1).astype(F32)
    rel = row - col
    row_v = lax.broadcasted_iota(jnp.int32, (c, dv), 0).astype(F32)
    tables = []
    for hd in heads:
        lg = lg_ref[hd]
        decay = jnp.where(rel >= 0, jnp.exp(jnp.maximum(rel, 0.0) * lg[:, :c]), 0.0)
        w_k = jnp.exp((c - 1.0 - row) * lg[:, :c])
        w_q = jnp.exp((row_v + 1.0) * lg)
        chunk_decay = jnp.exp(float(c) * lg)
        tables.append((decay, w_k, w_q, chunk_decay))

    def rotary(t, cos, sin):
        return t * cos + pltpu.roll(t, dk // 2, axis=1) * sin

    state_ref[...] = jnp.zeros(state_ref.shape, F32)

    def body(nn, carry):
        work = {}
        for t in range(RET_CHUNKS_PER_TRIP):
            for hd in heads:
                _, w_k, _, _ = tables[hd]
                rows = pl.ds(pl.multiple_of((nn * RET_CHUNKS_PER_TRIP + t) * c, c), c)
                kcols = slice(hd * dk, (hd + 1) * dk)
                vcols = slice(hd * dv, (hd + 1) * dv)
                cos, sin = cos_ref[rows, :], sin_ref[rows, :]
                qb = rotary(q_ref[rows, kcols], cos, sin).astype(BF16)
                kr = rotary(k_ref[rows, kcols], cos, sin) * (dk ** -0.5)
                vb = v_ref[rows, vcols]
                s = lax.dot_general(qb, kr.astype(BF16), (((1,), (1,)), ((), ())), preferred_element_type=F32)
                kv = jnp.dot((kr * w_k).T.astype(BF16), vb, preferred_element_type=F32)
                work[t, hd] = (rows, vcols, qb, vb, s, kv)
        outs = {}
        for hd in heads:
            decay, _, w_q, chunk_decay = tables[hd]
            state = state_ref[hd]
            for t in range(RET_CHUNKS_PER_TRIP):
                rows, vcols, qb, vb, s, kv = work[t, hd]
                outs[t, hd] = (jnp.dot((s * decay).astype(BF16), vb, preferred_element_type=F32)
                               + jnp.dot(qb, state.astype(BF16), preferred_element_type=F32) * w_q)
                state = chunk_decay * state + kv
            state_ref[hd] = state
        for (t, hd), o in outs.items():
            rows, vcols = work[t, hd][:2]
            mu = jnp.mean(o, axis=-1, keepdims=True)
            var = jnp.mean(jnp.square(o - mu), axis=-1, keepdims=True)
            on = (o - mu) * lax.rsqrt(var + 1e-5)
            o_ref[rows, vcols] = (g_ref[rows, vcols].astype(F32) * on).astype(o_ref.dtype)
        return carry

    lax.fori_loop(0, n_chunks // RET_CHUNKS_PER_TRIP, body, 0)


def _retention(ub3, uf3, lg, cos, sin):
    bn, s_len, _ = uf3.shape
    dk, dv = RET_KEY_DIM, RET_VAL_DIM
    hp = RET_HEADS_PER_STEP
    assert s_len % (RET_CHUNK * RET_CHUNKS_PER_TRIP) == 0 and RET_HEADS % hp == 0
    wk, wv = hp * dk, hp * dv
    return pl.pallas_call(
        _retention_kernel,
        grid=(bn, RET_HEADS // hp),
        in_specs=[
            pl.BlockSpec((hp, 1, dv), lambda b, h: (h, 0, 0)),
            pl.BlockSpec((s_len, dk), lambda b, h: (0, 0)),
            pl.BlockSpec((s_len, dk), lambda b, h: (0, 0)),
            pl.BlockSpec((None, s_len, wk), lambda b, h: (b, 0, COLF_RQ // wk + h)),
            pl.BlockSpec((None, s_len, wk), lambda b, h: (b, 0, COLF_RK // wk + h)),
            pl.BlockSpec((None, s_len, wv), lambda b, h: (b, 0, COLB_RV // wv + h)),
            pl.BlockSpec((None, s_len, wv), lambda b, h: (b, 0, COLB_RG // wv + h)),
        ],
        out_specs=pl.BlockSpec((None, s_len, wv), lambda b, h: (b, 0, h)),
        out_shape=jax.ShapeDtypeStruct((bn, s_len, RET_HEADS * dv), BF16),
        scratch_shapes=[pltpu.VMEM((hp, dk, dv), F32)],
        compiler_params=_params("parallel", "parallel"),
        name="retention",
    )(lg, cos, sin, uf3, uf3, ub3, ub3)


def _softplus(z):
    return jnp.maximum(z, 0.0) + jnp.log1p(jnp.exp(-jnp.abs(z)))


def _rglru_kernel(cw_ref, cb_ref, wa_ref, ba_ref, wx_ref, bx_ref, lam_ref, x_ref, y_ref, o_ref,
                  xpad_ref, hs_ref, h_ref):
    ts = x_ref.shape[0]
    halo = SUBLANES
    groups = ts // SUBLANES

    @pl.when(pl.program_id(1) == 0)
    def _():
        xpad_ref[:, 0:halo, :] = jnp.zeros((LRU_BLOCKS, halo, LRU_BLOCK_W), F32)
        h_ref[...] = jnp.zeros(h_ref.shape, F32)

    @pl.when(pl.program_id(1) > 0)
    def _():
        xpad_ref[:, 0:halo, :] = xpad_ref[:, ts:ts + halo, :]

    row = lax.broadcasted_iota(jnp.int32, (groups, LRU_BLOCK_W), 0)

    for n in range(LRU_BLOCKS):
        cols = slice(n * LRU_BLOCK_W, (n + 1) * LRU_BLOCK_W)
        xpad_ref[n, halo:halo + ts, :] = x_ref[:, cols]
        phase = {p: xpad_ref[n, pl.ds(halo + p, groups, stride=SUBLANES), :]
                 for p in range(1 - CONV_WIDTH, SUBLANES)}
        xcs = []
        for p in range(SUBLANES):
            xc = cb_ref[:, cols]
            for j in range(CONV_WIDTH):
                xc = xc + cw_ref[j:j + 1, cols] * phase[p + j - (CONV_WIDTH - 1)]
            xcs.append(xc)
        xc = jnp.concatenate(xcs, axis=0)
        xb = xc.astype(BF16)
        r = _sigmoid(jnp.dot(xb, wa_ref[n], preferred_element_type=F32) + ba_ref[:, cols])
        gi = _sigmoid(jnp.dot(xb, wx_ref[n], preferred_element_type=F32) + bx_ref[:, cols])
        log_a = (-LRU_C * _softplus(-lam_ref[:, cols])) * r
        a = jnp.exp(log_a)
        b = jnp.sqrt(1.0 - a * a) * (gi * xc)

        a_run, b_run = a[0:groups], b[0:groups]
        a_runs, b_runs = [a_run], [b_run]
        for p in range(1, SUBLANES):
            a_p = a[p * groups:(p + 1) * groups]
            b_run = a_p * b_run + b[p * groups:(p + 1) * groups]
            a_run = a_p * a_run
            a_runs.append(a_run)
            b_runs.append(b_run)
        h0 = h_ref[:, cols]
        end = jnp.where(row == 0, b_run + a_run * h0, b_run)
        prod = a_run
        d = 1
        while d < groups:
            keep = row >= d
            end = jnp.where(keep, prod * pltpu.roll(end, d, axis=0) + end, end)
            prod = jnp.where(keep, prod * pltpu.roll(prod, d, axis=0), prod)
            d *= 2
        h_in = jnp.where(row == 0, h0, pltpu.roll(end, 1, axis=0))
        h_ref[:, cols] = end[groups - 1:groups, :]
        for p in range(SUBLANES):
            hs_ref[n, pl.ds(p, groups, stride=SUBLANES), :] = b_runs[p] + a_runs[p] * h_in
        o_ref[:, cols] = (hs_ref[n] * y_ref[:, cols].astype(F32)).astype(o_ref.dtype)


def _rglru(ub3, uf3, conv_w, conv_b, wa, ba, wx, bx, lam, layer):
    bn, s_len, _ = uf3.shape
    w = LRU_BLOCKS * LRU_BLOCK_W
    ts = min(LRU_TS, s_len)
    vec = lambda: pl.BlockSpec((None, 1, w), lambda b, t: (layer, 0, 0))
    blk = lambda: pl.BlockSpec((None, LRU_BLOCKS, LRU_BLOCK_W, LRU_BLOCK_W), lambda b, t: (layer, 0, 0, 0))
    return pl.pallas_call(
        _rglru_kernel,
        grid=(bn, s_len // ts),
        in_specs=[
            pl.BlockSpec((None, CONV_WIDTH, w), lambda b, t: (layer, 0, 0)),
            vec(), blk(), vec(), blk(), vec(), vec(),
            pl.BlockSpec((None, ts, w), lambda b, t: (b, t, COLF_LX // w)),
            pl.BlockSpec((None, ts, w), lambda b, t: (b, t, COLB_LY // w)),
        ],
        out_specs=pl.BlockSpec((None, ts, w), lambda b, t: (b, t, 0)),
        out_shape=jax.ShapeDtypeStruct((bn, s_len, w), BF16),
        scratch_shapes=[pltpu.VMEM((LRU_BLOCKS, ts + SUBLANES, LRU_BLOCK_W), F32),
                        pltpu.VMEM((LRU_BLOCKS, ts, LRU_BLOCK_W), F32), pltpu.VMEM((1, w), F32)],
        compiler_params=_params("parallel", "arbitrary"),
        name="rglru",
    )(conv_w, conv_b, wa, ba, wx, bx, lam, uf3, ub3)


def _merge_kernel(od_ref, or_ref, ol_ref, g0_ref, g1_ref, g2_ref, x_ref, wb_ref, wo_ref, o_ref):
    merged = (g0_ref[...].astype(F32) * jnp.dot(od_ref[...], wb_ref[0], preferred_element_type=F32)
              + g1_ref[...].astype(F32) * jnp.dot(or_ref[...], wb_ref[1], preferred_element_type=F32)
              + g2_ref[...].astype(F32) * jnp.dot(ol_ref[...], wb_ref[2], preferred_element_type=F32))
    o_ref[...] = x_ref[...] + jnp.dot(merged.astype(BF16), wo_ref[...], preferred_element_type=F32)


def _merge(od, o_r, o_l, u2, x2, w_branch, w_out, layer, tm=512):
    m, d = x2.shape
    tm = min(tm, m)
    row = lambda: pl.BlockSpec((tm, d), lambda i: (i, 0))
    gate = lambda k: pl.BlockSpec((tm, d), lambda i: (i, COLB_GATES // d + k))
    return pl.pallas_call(
        _merge_kernel,
        grid=(m // tm,),
        in_specs=[
            row(), row(), row(), gate(0), gate(1), gate(2), row(),
            pl.BlockSpec((None, 3, d, d), lambda i: (layer, 0, 0, 0)),
            pl.BlockSpec((None, d, d), lambda i: (layer, 0, 0)),
        ],
        out_specs=row(),
        out_shape=jax.ShapeDtypeStruct((m, d), F32),
        compiler_params=_params("parallel"),
        name="merge",
    )(od, o_r, o_l, u2, u2, u2, x2, w_branch, w_out)


def _xattn_kernel(x_ref, g_ref, wq_ref, kv_ref, wo_ref, o_ref):
    x = x_ref[...]
    h = _rms_rows(x, g_ref[...], 1e-6).astype(BF16)
    q = jnp.dot(h, wq_ref[...], preferred_element_type=F32).astype(BF16)
    heads = []
    for hd in range(XA_HEADS):
        kc = slice(hd * XA_HEAD_DIM, (hd + 1) * XA_HEAD_DIM)
        vc = slice(XA_WIDTH + hd * XA_HEAD_DIM, XA_WIDTH + (hd + 1) * XA_HEAD_DIM)
        s = lax.dot_general(q[:, kc], kv_ref[:, kc], (((1,), (1,)), ((), ())),
                            preferred_element_type=F32) * (XA_HEAD_DIM ** -0.5)
        e = jnp.exp(s - jnp.max(s, axis=-1, keepdims=True))
        pv = jnp.dot(e.astype(BF16), kv_ref[:, vc], preferred_element_type=F32)
        heads.append((pv / jnp.sum(e, axis=-1, keepdims=True)).astype(BF16))
    o = jnp.concatenate(heads, axis=-1)
    o_ref[...] = x + jnp.dot(o, wo_ref[...], preferred_element_type=F32)


def _xattn(x3, g, wq, kv3, wo, layer, tm=1024):
    bn, s_len, d = x3.shape
    tm = min(tm, s_len)
    m_len = kv3.shape[1]
    return pl.pallas_call(
        _xattn_kernel,
        grid=(bn, s_len // tm),
        in_specs=[
            pl.BlockSpec((None, tm, d), lambda b, i: (b, i, 0)),
            pl.BlockSpec((None, 1, d), lambda b, i: (layer, 0, 0)),
            pl.BlockSpec((None, d, XA_WIDTH), lambda b, i: (layer, 0, 0)),
            pl.BlockSpec((None, m_len, 2 * XA_WIDTH), lambda b, i: (b, 0, 0)),
            pl.BlockSpec((None, XA_WIDTH, d), lambda b, i: (layer, 0, 0)),
        ],
        out_specs=pl.BlockSpec((None, tm, d), lambda b, i: (b, i, 0)),
        out_shape=jax.ShapeDtypeStruct((bn, s_len, d), F32),
        compiler_params=_params("parallel", "parallel"),
        name="xattn",
    )(x3, g, wq, kv3, wo)


def _mlp_kernel(x_ref, g_ref, w1_ref, w2_ref, gf_ref, o_ref, *, final_norm):
    x = x_ref[...]
    h = _rms_rows(x, g_ref[...], 1e-6).astype(BF16)
    d = x.shape[1]
    acc = x
    for c in range(w1_ref.shape[1] // d):
        cols = slice(c * d, (c + 1) * d)
        a = jnp.maximum(jnp.dot(h, w1_ref[:, cols], preferred_element_type=F32), 0.0)
        acc = acc + jnp.dot((a * a).astype(BF16), w2_ref[cols, :], preferred_element_type=F32)
    o_ref[...] = _rms_rows(acc, gf_ref[...], 1e-6) if final_norm else acc


def _mlp(x2, g, w1, w2, g_final, layer, final_norm, tm=512):
    m, d = x2.shape
    tm = min(tm, m)
    dff = w1.shape[2]
    return pl.pallas_call(
        functools.partial(_mlp_kernel, final_norm=final_norm),
        grid=(m // tm,),
        in_specs=[
            pl.BlockSpec((tm, d), lambda i: (i, 0)),
            pl.BlockSpec((None, 1, d), lambda i: (layer, 0, 0)),
            pl.BlockSpec((None, d, dff), lambda i: (layer, 0, 0)),
            pl.BlockSpec((None, dff, d), lambda i: (layer, 0, 0)),
            pl.BlockSpec((1, d), lambda i: (0, 0)),
        ],
        out_specs=pl.BlockSpec((tm, d), lambda i: (i, 0)),
        out_shape=jax.ShapeDtypeStruct((m, d), F32),
        compiler_params=_params("parallel"),
        name="mlp",
    )(x2, g, w1, w2, g_final)


def kernel(x, mem, norm_mix, w_in, diff_lq1, diff_lk1, diff_lq2, diff_lk2, diff_subln, lru_conv_w, lru_conv_b, lru_wa, lru_ba, lru_wx, lru_bx, lru_lambda, w_branch, w_out, norm_xattn, norm_mem, xa_wq, xa_wkv, xa_wo, norm_mlp, mlp_w1, mlp_w2, norm_final):
    bn, s_len, d = x.shape
    m_len = mem.shape[1]
    depth = norm_mix.shape[0]
    rows = bn * s_len

    qk0, qk1 = 3 * PROJ_TN, 4 * PROJ_TN
    w_qk = w_in[:, :, qk0:qk1].reshape(depth, d, (qk1 - qk0) // RET_KEY_DIM, RET_KEY_DIM // 2, 2)
    w_qk = jnp.swapaxes(w_qk, -1, -2).reshape(depth, d, qk1 - qk0)
    w_in_b = lax.dynamic_update_slice(w_in.astype(BF16), w_qk.astype(BF16), (0, 0, qk0))
    w_branch_b, w_out_b = w_branch.astype(BF16), w_out.astype(BF16)
    wq_b, wkv_b, wo_b = xa_wq.astype(BF16), xa_wkv.astype(BF16), xa_wo.astype(BF16)
    w1_b, w2_b = mlp_w1.astype(BF16), mlp_w2.astype(BF16)
    wa_b, wx_b = lru_wa.astype(BF16), lru_wx.astype(BF16)
    row3 = lambda p: p.reshape(depth, 1, p.shape[-1])
    subln_lanes = jnp.broadcast_to(diff_subln[:, :, None], (depth, DIFF_VAL_DIM, ATT_TQ))
    conv_w = lru_conv_w.reshape(depth, CONV_WIDTH, d)

    pos = jnp.arange(s_len, dtype=F32)
    angle = 1.0 / (10000.0 ** jnp.linspace(0.0, 1.0, RET_KEY_DIM // 2, dtype=F32))
    phase = pos[:, None] * angle[None, :]
    cos = jnp.concatenate([jnp.cos(phase), jnp.cos(phase)], axis=1)
    sin = jnp.concatenate([-jnp.sin(phase), jnp.sin(phase)], axis=1)
    log_g = jnp.log(1.0 - jnp.exp2(-5.0 - jnp.arange(RET_HEADS, dtype=F32)))
    lg = jnp.broadcast_to(log_g[:, None, None], (RET_HEADS, 1, RET_VAL_DIM))
    lam_inits = [0.8 - 0.6 * math.exp(-0.3 * l) for l in range(depth)]
    cst = jnp.zeros((depth, 1, 128), F32)
    cst = cst.at[:, 0, 0].set(jnp.asarray(lam_inits, F32))
    cst = cst.at[:, 0, 1].set(jnp.asarray([1.0 - v for v in lam_inits], F32))

    x2 = x.reshape(rows, d)
    mem2 = mem.reshape(bn * m_len, d)
    for l in range(depth):
        ub2, uf2 = _proj_in(x2, row3(norm_mix), w_in_b, l)
        ub3 = ub2.reshape(bn, s_len, ub2.shape[1])
        uf3 = uf2.reshape(bn, s_len, uf2.shape[1])
        od = _diff_attn(ub3, row3(diff_lq1), row3(diff_lk1), row3(diff_lq2), row3(diff_lk2), cst,
                        subln_lanes, l)
        o_r = _retention(ub3, uf3, lg, cos, sin)
        o_l = _rglru(ub3, uf3, conv_w, row3(lru_conv_b), wa_b, row3(lru_ba), wx_b, row3(lru_bx),
                     row3(lru_lambda), l)
        x2 = _merge(od.reshape(rows, d), o_r.reshape(rows, d), o_l.reshape(rows, d), ub2, x2,
                    w_branch_b, w_out_b, l)
        kv = _norm_matmul(mem2, row3(norm_mem), wkv_b, l, 2 * XA_WIDTH, BF16, tm=512, tn=2 * XA_WIDTH)
        x2 = _xattn(x2.reshape(bn, s_len, d), row3(norm_xattn), wq_b, kv.reshape(bn, m_len, 2 * XA_WIDTH),
                    wo_b, l).reshape(rows, d)
        x2 = _mlp(x2, row3(norm_mlp), w1_b, w2_b, norm_final.reshape(1, d), l, final_norm=(l == depth - 1))
    return x2.reshape(bn, s_len, d)
```

```python
import functools
import math

import jax
import jax.numpy as jnp
from jax import lax
from jax.experimental import pallas as pl
from jax.experimental.pallas import tpu as pltpu

F32 = jnp.float32
BF16 = jnp.bfloat16

D_MODEL = 1024
DIFF_HEADS = 8
DIFF_HEAD_DIM = 64
DIFF_VAL_DIM = 2 * DIFF_HEAD_DIM
RET_HEADS = 4
RET_KEY_DIM = 128
RET_VAL_DIM = 256
RET_CHUNK = 128
LRU_BLOCKS = 8
LRU_BLOCK_W = 128
CONV_WIDTH = 4
LRU_C = 8.0
XA_HEADS = 4
XA_HEAD_DIM = 128
XA_WIDTH = XA_HEADS * XA_HEAD_DIM
D_FF = 4 * D_MODEL

PROJ_TN = 1024
PROJ_CHUNK = 256
PROJ_TILES = (("dq", "qscale"), ("dk", "b"), ("dv", "b"), ("rqk", "f"), ("rv", "b"), ("rg", "silu"),
              ("lx", "f"), ("ly", "gelu"), ("g0", "sigmoid"), ("g1", "sigmoid"), ("g2", "sigmoid"))
PROJ_KINDS = tuple(kind for _, kind in PROJ_TILES)


def _proj_col(name):
    names = [n for n, _ in PROJ_TILES]
    kind = PROJ_KINDS[names.index(name)]
    same = [n for n, k in PROJ_TILES if (k == "f") == (kind == "f")]
    return same.index(name) * PROJ_TN


COLB_DQ, COLB_DK, COLB_DV, COLB_RV = (_proj_col(n) for n in ("dq", "dk", "dv", "rv"))
COLB_RG, COLB_LY, COLB_GATES = (_proj_col(n) for n in ("rg", "ly", "g0"))
COLF_RQ, COLF_LX = _proj_col("rqk"), _proj_col("lx")
COLF_RK = COLF_RQ + RET_HEADS * RET_KEY_DIM

VMEM_LIMIT_BYTES = 56 * 1024 * 1024

ATT_TQ = 256
ATT_TK = 256
ATT_HEADS_PER_STEP = 8
RET_CHUNKS_PER_TRIP = 4
RET_HEADS_PER_STEP = 2
LRU_TS = 512
SUBLANES = 8


def _params(*sem):
    return pltpu.CompilerParams(dimension_semantics=sem, vmem_limit_bytes=VMEM_LIMIT_BYTES)


def _rms_rows(xf, g, eps):
    return xf * lax.rsqrt(jnp.mean(xf * xf, axis=-1, keepdims=True) + eps) * g


def _norm_matmul_kernel(x_ref, g_ref, w_ref, o_ref, h_ref):
    @pl.when(pl.program_id(1) == 0)
    def _():
        h_ref[...] = _rms_rows(x_ref[...], g_ref[...], 1e-6).astype(BF16)

    o_ref[...] = jnp.dot(h_ref[...], w_ref[...], preferred_element_type=F32).astype(o_ref.dtype)


def _norm_matmul(x2, g, w, layer, n_cols, out_dtype, tm, tn):
    m, d = x2.shape
    tm = min(tm, m)
    return pl.pallas_call(
        _norm_matmul_kernel,
        grid=(m // tm, n_cols // tn),
        in_specs=[
            pl.BlockSpec((tm, d), lambda i, j: (i, 0)),
            pl.BlockSpec((None, 1, d), lambda i, j: (layer, 0, 0)),
            pl.BlockSpec((None, d, tn), lambda i, j: (layer, 0, j)),
        ],
        out_specs=pl.BlockSpec((tm, tn), lambda i, j: (i, j)),
        out_shape=jax.ShapeDtypeStruct((m, n_cols), out_dtype),
        scratch_shapes=[pltpu.VMEM((tm, d), BF16)],
        compiler_params=_params("parallel", "arbitrary"),
        name="norm_matmul",
    )(x2, g, w)


def _held_index(is_f32):
    table, n = [], 0
    for kind in PROJ_KINDS:
        if (kind == "f") == is_f32:
            n += 1
        table.append(max(n - 1, 0))
    steps = [j for j in range(1, len(table)) if table[j] != table[j - 1]]
    return lambda j: sum((j >= s).astype(jnp.int32) for s in steps)


def _gelu_tanh(y):
    c = math.sqrt(2.0 / math.pi)
    half = 0.5 * y
    return half + half * jnp.tanh(y * (c + (c * 0.044715) * (y * y)))


def _sigmoid(z):
    return 0.5 * jnp.tanh(0.5 * z) + 0.5


def _silu(z):
    half = 0.5 * z
    return half + half * jnp.tanh(half)


_PROJ_EPILOGUES = {
    "f": lambda u: u,
    "b": lambda u: u,
    "qscale": lambda u: u * (DIFF_HEAD_DIM ** -0.5 * math.log2(math.e)),
    "silu": _silu,
    "gelu": _gelu_tanh,
    "sigmoid": _sigmoid,
}


def _proj_in_kernel(x_ref, g_ref, w_ref, ob_ref, of_ref, h_ref):
    j = pl.program_id(1)

    @pl.when(j == 0)
    def _():
        h_ref[...] = _rms_rows(x_ref[...], g_ref[...], 1e-6).astype(BF16)

    for kind in sorted(set(PROJ_KINDS)):
        is_kind = functools.reduce(jnp.logical_or, [j == t for t, k in enumerate(PROJ_KINDS) if k == kind])
        out_ref = of_ref if kind == "f" else ob_ref

        @pl.when(is_kind)
        def _(kind=kind, out_ref=out_ref):
            for r in range(0, h_ref.shape[0], PROJ_CHUNK):
                u = jnp.dot(h_ref[r:r + PROJ_CHUNK, :], w_ref[...], preferred_element_type=F32)
                out_ref[r:r + PROJ_CHUNK, :] = _PROJ_EPILOGUES[kind](u).astype(out_ref.dtype)


def _proj_in(x2, g, w, layer, tm=2048):
    m, d = x2.shape
    tm = min(tm, m)
    tn = PROJ_TN
    idx_b, idx_f = _held_index(False), _held_index(True)
    n_f32 = PROJ_KINDS.count("f")
    return pl.pallas_call(
        _proj_in_kernel,
        grid=(m // tm, len(PROJ_KINDS)),
        in_specs=[
            pl.BlockSpec((tm, d), lambda i, j: (i, 0)),
            pl.BlockSpec((None, 1, d), lambda i, j: (layer, 0, 0)),
            pl.BlockSpec((None, d, tn), lambda i, j: (layer, 0, j)),
        ],
        out_specs=[
            pl.BlockSpec((tm, tn), lambda i, j: (i, idx_b(j))),
            pl.BlockSpec((tm, tn), lambda i, j: (i, idx_f(j))),
        ],
        out_shape=[
            jax.ShapeDtypeStruct((m, tn * (len(PROJ_KINDS) - n_f32)), BF16),
            jax.ShapeDtypeStruct((m, tn * n_f32), F32),
        ],
        scratch_shapes=[pltpu.VMEM((tm, d), BF16)],
        compiler_params=_params("parallel", "arbitrary"),
        name="proj_in",
    )(x2, g, w)


def _diff_attn_kernel(lq1_ref, lk1_ref, lq2_ref, lk2_ref, cst_ref, subln_ref, q_ref, k_ref, v_ref,
                      o_ref, qq_ref, vt_ref, s_ref, m_ref, l_ref, acc_ref):
    i = pl.program_id(2)
    tq, tk, wv = ATT_TQ, ATT_TK, DIFF_VAL_DIM
    n_kv = k_ref.shape[0] // tk
    heads = range(ATT_HEADS_PER_STEP)

    @pl.when(i == 0)
    def _():
        for hd in heads:
            for t in range(n_kv):
                v_t = v_ref[t * tk:(t + 1) * tk, hd * wv:(hd + 1) * wv]
                vt_ref[hd, t] = v_t.astype(F32).T.astype(BF16)

    feat = lax.broadcasted_iota(jnp.int32, (wv, tq), 0)
    for hd in heads:
        q_t = q_ref[:, hd * wv:(hd + 1) * wv].astype(F32).T
        qq_ref[hd, :, 0:tq] = jnp.where(feat < DIFF_HEAD_DIM, q_t, 0.0).astype(BF16)
        qq_ref[hd, :, tq:2 * tq] = jnp.where(feat >= DIFF_HEAD_DIM, q_t, 0.0).astype(BF16)

    m_ref[...] = jnp.full(m_ref.shape, -jnp.inf, F32)
    l_ref[...] = jnp.zeros(l_ref.shape, F32)
    acc_ref[...] = jnp.zeros(acc_ref.shape, F32)

    def scores(j, hd):
        k_t = k_ref[pl.ds(pl.multiple_of(j * tk, tk), tk), hd * wv:(hd + 1) * wv]
        s_ref[hd] = jnp.dot(k_t, qq_ref[hd], preferred_element_type=F32)

    def accumulate(j, hd, masked):
        s = s_ref[hd]
        if masked:
            key = lax.broadcasted_iota(jnp.int32, (tk, tq), 0)
            qry = lax.broadcasted_iota(jnp.int32, (tk, tq), 1)
            ok = key <= qry
            s = jnp.where(jnp.concatenate([ok, ok], axis=1), s, -jnp.inf)
        m_old = m_ref[hd]
        m_new = jnp.maximum(m_old, jnp.max(s, axis=0, keepdims=True))
        alpha = jnp.exp2(m_old - m_new)
        e = jnp.exp2(s - m_new)
        l_ref[hd] = alpha * l_ref[hd] + jnp.sum(e, axis=0, keepdims=True)
        acc_ref[hd] = alpha * acc_ref[hd] + jnp.dot(vt_ref[hd, j], e.astype(BF16), preferred_element_type=F32)
        m_ref[hd] = m_new

    for hd in heads:
        scores(0, hd)

    def body(j, carry):
        for hd in heads:
            accumulate(j, hd, False)
            scores(j + 1, hd)
        return carry

    lax.fori_loop(0, i, body, 0)
    for hd in heads:
        accumulate(i, hd, True)

    lam_init = cst_ref[0:1, 0:1]
    out_scale = cst_ref[0:1, 1:2]
    lam = (jnp.exp(jnp.sum(lq1_ref[...] * lk1_ref[...], axis=-1, keepdims=True))
           - jnp.exp(jnp.sum(lq2_ref[...] * lk2_ref[...], axis=-1, keepdims=True)) + lam_init)
    for hd in heads:
        on = acc_ref[hd] * (1.0 / l_ref[hd])
        ot = on[:, :tq] - lam * on[:, tq:]
        inv = lax.rsqrt(jnp.mean(ot * ot, axis=0, keepdims=True) + 1e-5) * out_scale
        o_ref[:, hd * wv:(hd + 1) * wv] = (ot * inv * subln_ref[...]).T.astype(o_ref.dtype)


def _diff_attn(ub3, lq1, lk1, lq2, lk2, cst, subln, layer):
    bn, s_len, _ = ub3.shape
    tq, hp = ATT_TQ, ATT_HEADS_PER_STEP
    assert s_len % tq == 0 and DIFF_HEADS % hp == 0
    nq = s_len // tq
    wv = DIFF_VAL_DIM
    wb = hp * wv
    vec = lambda width: pl.BlockSpec((None, 1, width), lambda b, h, i: (layer, 0, 0))
    return pl.pallas_call(
        _diff_attn_kernel,
        grid=(bn, DIFF_HEADS // hp, nq),
        in_specs=[
            vec(DIFF_HEAD_DIM), vec(DIFF_HEAD_DIM), vec(DIFF_HEAD_DIM), vec(DIFF_HEAD_DIM),
            vec(128),
            pl.BlockSpec((None, wv, tq), lambda b, h, i: (layer, 0, 0)),
            pl.BlockSpec((None, tq, wb), lambda b, h, i: (b, i, COLB_DQ // wb + h)),
            pl.BlockSpec((None, s_len, wb), lambda b, h, i: (b, 0, COLB_DK // wb + h)),
            pl.BlockSpec((None, s_len, wb), lambda b, h, i: (b, 0, COLB_DV // wb + h)),
        ],
        out_specs=pl.BlockSpec((None, tq, wb), lambda b, h, i: (b, i, h)),
        out_shape=jax.ShapeDtypeStruct((bn, s_len, DIFF_HEADS * wv), BF16),
        scratch_shapes=[
            pltpu.VMEM((hp, wv, 2 * tq), BF16),
            pltpu.VMEM((hp, s_len // ATT_TK, wv, ATT_TK), BF16),
            pltpu.VMEM((hp, ATT_TK, 2 * tq), F32),
            pltpu.VMEM((hp, 1, 2 * tq), F32),
            pltpu.VMEM((hp, 1, 2 * tq), F32),
            pltpu.VMEM((hp, wv, 2 * tq), F32),
        ],
        compiler_params=_params("parallel", "parallel", "arbitrary"),
        name="diff_attn",
    )(lq1, lk1, lq2, lk2, cst, subln, ub3, ub3, ub3)


def _retention_kernel(lg_ref, cos_ref, sin_ref, q_ref, k_ref, v_ref, g_ref, o_ref, state_ref):
    c = RET_CHUNK
    dk, dv = RET_KEY_DIM, RET_VAL_DIM
    n_chunks = q_ref.shape[0] // c
    heads = range(RET_HEADS_PER_STEP)
    row = lax.broadcasted_iota(jnp.int32, (c, c), 0).astype(F32)
    col = lax.broadcasted_iota(jnp.int32, (c, c), 1).astype(F32)
    rel = row - col
    row_v = lax.broadcasted_iota(jnp.int32, (c, dv), 0).astype(F32)
    tables = []
    for hd in heads:
        lg = lg_ref[hd]
        decay = jnp.where(rel >= 0, jnp.exp(jnp.maximum(rel, 0.0) * lg[:, :c]), 0.0)
        w_k = jnp.exp((c - 1.0 - col) * lg[:, :c])
        w_q = jnp.exp((row_v + 1.0) * lg)
        chunk_decay = jnp.exp(float(c) * lg)
        tables.append((decay, w_k, w_q, chunk_decay))

    def rotary(t, cos, sin):
        return t * cos + pltpu.roll(t, dk // 2, axis=1) * sin

    state_ref[...] = jnp.zeros(state_ref.shape, F32)

    def body(nn, carry):
        work = {}
        for t in range(RET_CHUNKS_PER_TRIP):
            for hd in heads:
                _, w_k, _, _ = tables[hd]
                rows = pl.ds(pl.multiple_of((nn * RET_CHUNKS_PER_TRIP + t) * c, c), c)
                kcols = slice(hd * dk, (hd + 1) * dk)
                vcols = slice(hd * dv, (hd + 1) * dv)
                cos, sin = cos_ref[rows, :], sin_ref[rows, :]
                qb = rotary(q_ref[rows, kcols], cos, sin).astype(BF16)
                kr_t = (rotary(k_ref[rows, kcols], cos, sin) * (dk ** -0.5)).T
                vb = v_ref[rows, vcols]
                s = jnp.dot(qb, kr_t.astype(BF16), preferred_element_type=F32)
                kv = jnp.dot((kr_t * w_k).astype(BF16), vb, preferred_element_type=F32)
                work[t, hd] = (rows, vcols, qb, vb, s, kv)
        outs = {}
        for hd in heads:
            decay, _, w_q, chunk_decay = tables[hd]
            state = state_ref[hd]
            for t in range(RET_CHUNKS_PER_TRIP):
                rows, vcols, qb, vb, s, kv = work[t, hd]
                outs[t, hd] = (jnp.dot((s * decay).astype(BF16), vb, preferred_element_type=F32)
                               + jnp.dot(qb, state.astype(BF16), preferred_element_type=F32) * w_q)
                state = chunk_decay * state + kv
            state_ref[hd] = state
        for (t, hd), o in outs.items():
            rows, vcols = work[t, hd][:2]
            mu = jnp.mean(o, axis=-1, keepdims=True)
            var = jnp.mean(jnp.square(o - mu), axis=-1, keepdims=True)
            on = (o - mu) * lax.rsqrt(var + 1e-5)
            o_ref[rows, vcols] = (g_ref[rows, vcols].astype(F32) * on).astype(o_ref.dtype)
        return carry

    lax.fori_loop(0, n_chunks // RET_CHUNKS_PER_TRIP, body, 0)


def _retention(ub3, uf3, lg, cos, sin):
    bn, s_len, _ = uf3.shape
    dk, dv = RET_KEY_DIM, RET_VAL_DIM
    hp = RET_HEADS_PER_STEP
    assert s_len % (RET_CHUNK * RET_CHUNKS_PER_TRIP) == 0 and RET_HEADS % hp == 0
    wk, wv = hp * dk, hp * dv
    return pl.pallas_call(
        _retention_kernel,
        grid=(bn, RET_HEADS // hp),
        in_specs=[
            pl.BlockSpec((hp, 1, dv), lambda b, h: (h, 0, 0)),
            pl.BlockSpec((s_len, dk), lambda b, h: (0, 0)),
            pl.BlockSpec((s_len, dk), lambda b, h: (0, 0)),
            pl.BlockSpec((None, s_len, wk), lambda b, h: (b, 0, COLF_RQ // wk + h)),
            pl.BlockSpec((None, s_len, wk), lambda b, h: (b, 0, COLF_RK // wk + h)),
            pl.BlockSpec((None, s_len, wv), lambda b, h: (b, 0, COLB_RV // wv + h)),
            pl.BlockSpec((None, s_len, wv), lambda b, h: (b, 0, COLB_RG // wv + h)),
        ],
        out_specs=pl.BlockSpec((None, s_len, wv), lambda b, h: (b, 0, h)),
        out_shape=jax.ShapeDtypeStruct((bn, s_len, RET_HEADS * dv), BF16),
        scratch_shapes=[pltpu.VMEM((hp, dk, dv), F32)],
        compiler_params=_params("parallel", "parallel"),
        name="retention",
    )(lg, cos, sin, uf3, uf3, ub3, ub3)


def _softplus(z):
    return jnp.maximum(z, 0.0) + jnp.log1p(jnp.exp(-jnp.abs(z)))


def _rglru_kernel(cw_ref, cb_ref, wa_ref, ba_ref, wx_ref, bx_ref, lam_ref, x_ref, y_ref, o_ref,
                  xpad_ref, hs_ref, h_ref):
    ts = x_ref.shape[0]
    halo = SUBLANES
    groups = ts // SUBLANES

    @pl.when(pl.program_id(1) == 0)
    def _():
        xpad_ref[:, 0:halo, :] = jnp.zeros((LRU_BLOCKS, halo, LRU_BLOCK_W), F32)
        h_ref[...] = jnp.zeros(h_ref.shape, F32)

    @pl.when(pl.program_id(1) > 0)
    def _():
        xpad_ref[:, 0:halo, :] = xpad_ref[:, ts:ts + halo, :]

    row = lax.broadcasted_iota(jnp.int32, (groups, LRU_BLOCK_W), 0)

    for n in range(LRU_BLOCKS):
        cols = slice(n * LRU_BLOCK_W, (n + 1) * LRU_BLOCK_W)
        xpad_ref[n, halo:halo + ts, :] = x_ref[:, cols]
        phase = {p: xpad_ref[n, pl.ds(halo + p, groups, stride=SUBLANES), :]
                 for p in range(1 - CONV_WIDTH, SUBLANES)}
        xcs = []
        for p in range(SUBLANES):
            xc = cb_ref[:, cols]
            for j in range(CONV_WIDTH):
                xc = xc + cw_ref[j:j + 1, cols] * phase[p + j - (CONV_WIDTH - 1)]
            xcs.append(xc)
        xc = jnp.concatenate(xcs, axis=0)
        xb = xc.astype(BF16)
        r = _sigmoid(jnp.dot(xb, wa_ref[n], preferred_element_type=F32) + ba_ref[:, cols])
        gi = _sigmoid(jnp.dot(xb, wx_ref[n], preferred_element_type=F32) + bx_ref[:, cols])
        log_a = (-LRU_C * _softplus(-lam_ref[:, cols])) * r
        a = jnp.exp(log_a)
        b = jnp.sqrt(1.0 - a * a) * (gi * xc)

        a_run, b_run = a[0:groups], b[0:groups]
        a_runs, b_runs = [a_run], [b_run]
        for p in range(1, SUBLANES):
            a_p = a[p * groups:(p + 1) * groups]
            b_run = a_p * b_run + b[p * groups:(p + 1) * groups]
            a_run = a_p * a_run
            a_runs.append(a_run)
            b_runs.append(b_run)
        h0 = h_ref[:, cols]
        end = jnp.where(row == 0, b_run + a_run * h0, b_run)
        prod = a_run
        d = 1
        while d < groups:
            keep = row >= d
            end = jnp.where(keep, prod * pltpu.roll(end, d, axis=0) + end, end)
            prod = jnp.where(keep, prod * pltpu.roll(prod, d, axis=0), prod)
            d *= 2
        h_in = jnp.where(row == 0, h0, pltpu.roll(end, 1, axis=0))
        h_ref[:, cols] = end[groups - 1:groups, :]
        for p in range(SUBLANES):
            hs_ref[n, pl.ds(p, groups, stride=SUBLANES), :] = b_runs[p] + a_runs[p] * h_in
        o_ref[:, cols] = (hs_ref[n] * y_ref[:, cols].astype(F32)).astype(o_ref.dtype)


def _rglru(ub3, uf3, conv_w, conv_b, wa, ba, wx, bx, lam, layer):
    bn, s_len, _ = uf3.shape
    w = LRU_BLOCKS * LRU_BLOCK_W
    ts = min(LRU_TS, s_len)
    vec = lambda: pl.BlockSpec((None, 1, w), lambda b, t: (layer, 0, 0))
    blk = lambda: pl.BlockSpec((None, LRU_BLOCKS, LRU_BLOCK_W, LRU_BLOCK_W), lambda b, t: (layer, 0, 0, 0))
    return pl.pallas_call(
        _rglru_kernel,
        grid=(bn, s_len // ts),
        in_specs=[
            pl.BlockSpec((None, CONV_WIDTH, w), lambda b, t: (layer, 0, 0)),
            vec(), blk(), vec(), blk(), vec(), vec(),
            pl.BlockSpec((None, ts, w), lambda b, t: (b, t, COLF_LX // w)),
            pl.BlockSpec((None, ts, w), lambda b, t: (b, t, COLB_LY // w)),
        ],
        out_specs=pl.BlockSpec((None, ts, w), lambda b, t: (b, t, 0)),
        out_shape=jax.ShapeDtypeStruct((bn, s_len, w), BF16),
        scratch_shapes=[pltpu.VMEM((LRU_BLOCKS, ts + SUBLANES, LRU_BLOCK_W), F32),
                        pltpu.VMEM((LRU_BLOCKS, ts, LRU_BLOCK_W), F32), pltpu.VMEM((1, w), F32)],
        compiler_params=_params("parallel", "arbitrary"),
        name="rglru",
    )(conv_w, conv_b, wa, ba, wx, bx, lam, uf3, ub3)


def _merge_kernel(od_ref, or_ref, ol_ref, g0_ref, g1_ref, g2_ref, x_ref, wb_ref, wo_ref, o_ref):
    merged = (g0_ref[...].astype(F32) * jnp.dot(od_ref[...], wb_ref[0], preferred_element_type=F32)
              + g1_ref[...].astype(F32) * jnp.dot(or_ref[...], wb_ref[1], preferred_element_type=F32)
              + g2_ref[...].astype(F32) * jnp.dot(ol_ref[...], wb_ref[2], preferred_element_type=F32))
    o_ref[...] = x_ref[...] + jnp.dot(merged.astype(BF16), wo_ref[...], preferred_element_type=F32)


def _merge(od, o_r, o_l, u2, x2, w_branch, w_out, layer, tm=512):
    m, d = x2.shape
    tm = min(tm, m)
    row = lambda: pl.BlockSpec((tm, d), lambda i: (i, 0))
    gate = lambda k: pl.BlockSpec((tm, d), lambda i: (i, COLB_GATES // d + k))
    return pl.pallas_call(
        _merge_kernel,
        grid=(m // tm,),
        in_specs=[
            row(), row(), row(), gate(0), gate(1), gate(2), row(),
            pl.BlockSpec((None, 3, d, d), lambda i: (layer, 0, 0, 0)),
            pl.BlockSpec((None, d, d), lambda i: (layer, 0, 0)),
        ],
        out_specs=row(),
        out_shape=jax.ShapeDtypeStruct((m, d), F32),
        compiler_params=_params("parallel"),
        name="merge",
    )(od, o_r, o_l, u2, u2, u2, x2, w_branch, w_out)


def _xattn_kernel(x_ref, g_ref, wq_ref, kv_ref, wo_ref, o_ref):
    x = x_ref[...]
    h = _rms_rows(x, g_ref[...], 1e-6).astype(BF16)
    q = jnp.dot(h, wq_ref[...], preferred_element_type=F32).astype(BF16)
    k_t = kv_ref[:, 0:XA_WIDTH].astype(F32).T.astype(BF16)
    heads = []
    for hd in range(XA_HEADS):
        kc = slice(hd * XA_HEAD_DIM, (hd + 1) * XA_HEAD_DIM)
        vc = slice(XA_WIDTH + hd * XA_HEAD_DIM, XA_WIDTH + (hd + 1) * XA_HEAD_DIM)
        s = jnp.dot(q[:, kc], k_t[kc, :], preferred_element_type=F32) * (XA_HEAD_DIM ** -0.5)
        e = jnp.exp(s - jnp.max(s, axis=-1, keepdims=True))
        pv = jnp.dot(e.astype(BF16), kv_ref[:, vc], preferred_element_type=F32)
        heads.append((pv / jnp.sum(e, axis=-1, keepdims=True)).astype(BF16))
    o = jnp.concatenate(heads, axis=-1)
    o_ref[...] = x + jnp.dot(o, wo_ref[...], preferred_element_type=F32)


def _xattn(x3, g, wq, kv3, wo, layer, tm=1024):
    bn, s_len, d = x3.shape
    tm = min(tm, s_len)
    m_len = kv3.shape[1]
    return pl.pallas_call(
        _xattn_kernel,
        grid=(bn, s_len // tm),
        in_specs=[
            pl.BlockSpec((None, tm, d), lambda b, i: (b, i, 0)),
            pl.BlockSpec((None, 1, d), lambda b, i: (layer, 0, 0)),
            pl.BlockSpec((None, d, XA_WIDTH), lambda b, i: (layer, 0, 0)),
            pl.BlockSpec((None, m_len, 2 * XA_WIDTH), lambda b, i: (b, 0, 0)),
            pl.BlockSpec((None, XA_WIDTH, d), lambda b, i: (layer, 0, 0)),
        ],
        out_specs=pl.BlockSpec((None, tm, d), lambda b, i: (b, i, 0)),
        out_shape=jax.ShapeDtypeStruct((bn, s_len, d), F32),
        compiler_params=_params("parallel", "parallel"),
        name="xattn",
    )(x3, g, wq, kv3, wo)


def _mlp_kernel(x_ref, g_ref, w1_ref, w2_ref, gf_ref, o_ref, *, final_norm):
    x = x_ref[...]
    h = _rms_rows(x, g_ref[...], 1e-6).astype(BF16)
    d = x.shape[1]
    acc = x
    for c in range(w1_ref.shape[1] // d):
        cols = slice(c * d, (c + 1) * d)
        a = jnp.maximum(jnp.dot(h, w1_ref[:, cols], preferred_element_type=F32), 0.0)
        acc = acc + jnp.dot((a * a).astype(BF16), w2_ref[cols, :], preferred_element_type=F32)
    o_ref[...] = _rms_rows(acc, gf_ref[...], 1e-6) if final_norm else acc


def _mlp(x2, g, w1, w2, g_final, layer, final_norm, tm=512):
    m, d = x2.shape
    tm = min(tm, m)
    dff = w1.shape[2]
    return pl.pallas_call(
        functools.partial(_mlp_kernel, final_norm=final_norm),
        grid=(m // tm,),
        in_specs=[
            pl.BlockSpec((tm, d), lambda i: (i, 0)),
            pl.BlockSpec((None, 1, d), lambda i: (layer, 0, 0)),
            pl.BlockSpec((None, d, dff), lambda i: (layer, 0, 0)),
            pl.BlockSpec((None, dff, d), lambda i: (layer, 0, 0)),
            pl.BlockSpec((1, d), lambda i: (0, 0)),
        ],
        out_specs=pl.BlockSpec((tm, d), lambda i: (i, 0)),
        out_shape=jax.ShapeDtypeStruct((m, d), F32),
        compiler_params=_params("parallel"),
        name="mlp",
    )(x2, g, w1, w2, g_final)


def kernel(x, mem, norm_mix, w_in, diff_lq1, diff_lk1, diff_lq2, diff_lk2, diff_subln, lru_conv_w, lru_conv_b, lru_wa, lru_ba, lru_wx, lru_bx, lru_lambda, w_branch, w_out, norm_xattn, norm_mem, xa_wq, xa_wkv, xa_wo, norm_mlp, mlp_w1, mlp_w2, norm_final):
    bn, s_len, d = x.shape
    m_len = mem.shape[1]
    depth = norm_mix.shape[0]
    rows = bn * s_len

    qk0, qk1 = 3 * PROJ_TN, 4 * PROJ_TN
    w_qk = w_in[:, :, qk0:qk1].reshape(depth, d, (qk1 - qk0) // RET_KEY_DIM, RET_KEY_DIM // 2, 2)
    w_qk = jnp.swapaxes(w_qk, -1, -2).reshape(depth, d, qk1 - qk0)
    w_in_b = lax.dynamic_update_slice(w_in.astype(BF16), w_qk.astype(BF16), (0, 0, qk0))
    w_branch_b, w_out_b = w_branch.astype(BF16), w_out.astype(BF16)
    wq_b, wkv_b, wo_b = xa_wq.astype(BF16), xa_wkv.astype(BF16), xa_wo.astype(BF16)
    w1_b, w2_b = mlp_w1.astype(BF16), mlp_w2.astype(BF16)
    wa_b, wx_b = lru_wa.astype(BF16), lru_wx.astype(BF16)
    row3 = lambda p: p.reshape(depth, 1, p.shape[-1])
    subln_lanes = jnp.broadcast_to(diff_subln[:, :, None], (depth, DIFF_VAL_DIM, ATT_TQ))
    conv_w = lru_conv_w.reshape(depth, CONV_WIDTH, d)

    pos = jnp.arange(s_len, dtype=F32)
    angle = 1.0 / (10000.0 ** jnp.linspace(0.0, 1.0, RET_KEY_DIM // 2, dtype=F32))
    phase = pos[:, None] * angle[None, :]
    cos = jnp.concatenate([jnp.cos(phase), jnp.cos(phase)], axis=1)
    sin = jnp.concatenate([-jnp.sin(phase), jnp.sin(phase)], axis=1)
    log_g = jnp.log(1.0 - jnp.exp2(-5.0 - jnp.arange(RET_HEADS, dtype=F32)))
    lg = jnp.broadcast_to(log_g[:, None, None], (RET_HEADS, 1, RET_VAL_DIM))
    lam_inits = [0.8 - 0.6 * math.exp(-0.3 * l) for l in range(depth)]
    cst = jnp.zeros((depth, 1, 128), F32)
    cst = cst.at[:, 0, 0].set(jnp.asarray(lam_inits, F32))
    cst = cst.at[:, 0, 1].set(jnp.asarray([1.0 - v for v in lam_inits], F32))

    x2 = x.reshape(rows, d)
    mem2 = mem.reshape(bn * m_len, d)
    for l in range(depth):
        ub2, uf2 = _proj_in(x2, row3(norm_mix), w_in_b, l)
        ub3 = ub2.reshape(bn, s_len, ub2.shape[1])
        uf3 = uf2.reshape(bn, s_len, uf2.shape[1])
        od = _diff_attn(ub3, row3(diff_lq1), row3(diff_lk1), row3(diff_lq2), row3(diff_lk2), cst,
                        subln_lanes, l)
        o_r = _retention(ub3, uf3, lg, cos, sin)
        o_l = _rglru(ub3, uf3, conv_w, row3(lru_conv_b), wa_b, row3(lru_ba), wx_b, row3(lru_bx),
                     row3(lru_lambda), l)
        x2 = _merge(od.reshape(rows, d), o_r.reshape(rows, d), o_l.reshape(rows, d), ub2, x2,
                    w_branch_b, w_out_b, l)
        kv = _norm_matmul(mem2, row3(norm_mem), wkv_b, l, 2 * XA_WIDTH, BF16, tm=512, tn=2 * XA_WIDTH)
        x2 = _xattn(x2.reshape(bn, s_len, d), row3(norm_xattn), wq_b, kv.reshape(bn, m_len, 2 * XA_WIDTH),
                    wo_b, l).reshape(rows, d)
        x2 = _mlp(x2, row3(norm_mlp), w1_b, w2_b, norm_final.reshape(1, d), l, final_norm=(l == depth - 1))
    return x2.reshape(bn, s_len, d)
```

```python
import functools
import math

import jax
import jax.numpy as jnp
from jax import lax
from jax.experimental import pallas as pl
from jax.experimental.pallas import tpu as pltpu

F32 = jnp.float32
BF16 = jnp.bfloat16

D_MODEL = 1024
DIFF_HEADS = 8
DIFF_HEAD_DIM = 64
DIFF_VAL_DIM = 2 * DIFF_HEAD_DIM
RET_HEADS = 4
RET_KEY_DIM = 128
RET_VAL_DIM = 256
RET_CHUNK = 128
LRU_BLOCKS = 8
LRU_BLOCK_W = 128
CONV_WIDTH = 4
LRU_C = 8.0
XA_HEADS = 4
XA_HEAD_DIM = 128
XA_WIDTH = XA_HEADS * XA_HEAD_DIM
D_FF = 4 * D_MODEL

PROJ_TN = 1024
PROJ_CHUNK = 256
PROJ_TILES = (("dq", "qscale"), ("dk", "b"), ("dv", "b"), ("rqk", "f"), ("rv", "b"), ("rg", "silu"),
              ("lx", "f"), ("ly", "gelu"), ("g0", "sigmoid"), ("g1", "sigmoid"), ("g2", "sigmoid"))
PROJ_KINDS = tuple(kind for _, kind in PROJ_TILES)


def _proj_col(name):
    names = [n for n, _ in PROJ_TILES]
    kind = PROJ_KINDS[names.index(name)]
    same = [n for n, k in PROJ_TILES if (k == "f") == (kind == "f")]
    return same.index(name) * PROJ_TN


COLB_DQ, COLB_DK, COLB_DV, COLB_RV = (_proj_col(n) for n in ("dq", "dk", "dv", "rv"))
COLB_RG, COLB_LY, COLB_GATES = (_proj_col(n) for n in ("rg", "ly", "g0"))
COLF_RQ, COLF_LX = _proj_col("rqk"), _proj_col("lx")
COLF_RK = COLF_RQ + RET_HEADS * RET_KEY_DIM

VMEM_LIMIT_BYTES = 56 * 1024 * 1024

ATT_TQ = 512
ATT_TK = 256
ATT_HEADS_PER_STEP = 8
RET_CHUNKS_PER_TRIP = 4
RET_HEADS_PER_STEP = 2
LRU_TS = 512
SUBLANES = 8


def _params(*sem):
    return pltpu.CompilerParams(dimension_semantics=sem, vmem_limit_bytes=VMEM_LIMIT_BYTES)


def _rms_rows(xf, g, eps):
    return xf * lax.rsqrt(jnp.mean(xf * xf, axis=-1, keepdims=True) + eps) * g


def _norm_matmul_kernel(x_ref, g_ref, w_ref, o_ref, h_ref):
    @pl.when(pl.program_id(1) == 0)
    def _():
        h_ref[...] = _rms_rows(x_ref[...], g_ref[...], 1e-6).astype(BF16)

    o_ref[...] = jnp.dot(h_ref[...], w_ref[...], preferred_element_type=F32).astype(o_ref.dtype)


def _norm_matmul(x2, g, w, layer, n_cols, out_dtype, tm, tn):
    m, d = x2.shape
    tm = min(tm, m)
    return pl.pallas_call(
        _norm_matmul_kernel,
        grid=(m // tm, n_cols // tn),
        in_specs=[
            pl.BlockSpec((tm, d), lambda i, j: (i, 0)),
            pl.BlockSpec((None, 1, d), lambda i, j: (layer, 0, 0)),
            pl.BlockSpec((None, d, tn), lambda i, j: (layer, 0, j)),
        ],
        out_specs=pl.BlockSpec((tm, tn), lambda i, j: (i, j)),
        out_shape=jax.ShapeDtypeStruct((m, n_cols), out_dtype),
        scratch_shapes=[pltpu.VMEM((tm, d), BF16)],
        compiler_params=_params("parallel", "arbitrary"),
        name="norm_matmul",
    )(x2, g, w)


def _held_index(is_f32):
    table, n = [], 0
    for kind in PROJ_KINDS:
        if (kind == "f") == is_f32:
            n += 1
        table.append(max(n - 1, 0))
    steps = [j for j in range(1, len(table)) if table[j] != table[j - 1]]
    return lambda j: sum((j >= s).astype(jnp.int32) for s in steps)


def _gelu_tanh(y):
    c = math.sqrt(2.0 / math.pi)
    half = 0.5 * y
    return half + half * jnp.tanh(y * (c + (c * 0.044715) * (y * y)))


def _sigmoid(z):
    return 0.5 * jnp.tanh(0.5 * z) + 0.5


def _silu(z):
    half = 0.5 * z
    return half + half * jnp.tanh(half)


_PROJ_EPILOGUES = {
    "f": lambda u: u,
    "b": lambda u: u,
    "qscale": lambda u: u * (DIFF_HEAD_DIM ** -0.5 * math.log2(math.e)),
    "silu": _silu,
    "gelu": _gelu_tanh,
    "sigmoid": _sigmoid,
}


def _proj_in_kernel(x_ref, g_ref, w_ref, ob_ref, of_ref, h_ref):
    j = pl.program_id(1)

    @pl.when(j == 0)
    def _():
        h_ref[...] = _rms_rows(x_ref[...], g_ref[...], 1e-6).astype(BF16)

    for kind in sorted(set(PROJ_KINDS)):
        is_kind = functools.reduce(jnp.logical_or, [j == t for t, k in enumerate(PROJ_KINDS) if k == kind])
        out_ref = of_ref if kind == "f" else ob_ref

        @pl.when(is_kind)
        def _(kind=kind, out_ref=out_ref):
            for r in range(0, h_ref.shape[0], PROJ_CHUNK):
                u = jnp.dot(h_ref[r:r + PROJ_CHUNK, :], w_ref[...], preferred_element_type=F32)
                out_ref[r:r + PROJ_CHUNK, :] = _PROJ_EPILOGUES[kind](u).astype(out_ref.dtype)


def _proj_in(x2, g, w, layer, tm=2048):
    m, d = x2.shape
    tm = min(tm, m)
    tn = PROJ_TN
    idx_b, idx_f = _held_index(False), _held_index(True)
    n_f32 = PROJ_KINDS.count("f")
    return pl.pallas_call(
        _proj_in_kernel,
        grid=(m // tm, len(PROJ_KINDS)),
        in_specs=[
            pl.BlockSpec((tm, d), lambda i, j: (i, 0)),
            pl.BlockSpec((None, 1, d), lambda i, j: (layer, 0, 0)),
            pl.BlockSpec((None, d, tn), lambda i, j: (layer, 0, j)),
        ],
        out_specs=[
            pl.BlockSpec((tm, tn), lambda i, j: (i, idx_b(j))),
            pl.BlockSpec((tm, tn), lambda i, j: (i, idx_f(j))),
        ],
        out_shape=[
            jax.ShapeDtypeStruct((m, tn * (len(PROJ_KINDS) - n_f32)), BF16),
            jax.ShapeDtypeStruct((m, tn * n_f32), F32),
        ],
        scratch_shapes=[pltpu.VMEM((tm, d), BF16)],
        compiler_params=_params("parallel", "arbitrary"),
        name="proj_in",
    )(x2, g, w)


def _diff_attn_kernel(lq1_ref, lk1_ref, lq2_ref, lk2_ref, cst_ref, subln_ref, q_ref, k_ref, v_ref,
                      o_ref, qq_ref, vt_ref, s_ref, m_ref, l_ref, acc_ref):
    i = pl.program_id(2)
    tq, tk, wv = ATT_TQ, ATT_TK, DIFF_VAL_DIM
    n_kv = k_ref.shape[0] // tk
    heads = range(ATT_HEADS_PER_STEP)

    @pl.when(i == 0)
    def _():
        for hd in heads:
            for t in range(n_kv):
                v_t = v_ref[t * tk:(t + 1) * tk, hd * wv:(hd + 1) * wv]
                vt_ref[hd, t] = v_t.astype(F32).T.astype(BF16)

    lane = lax.broadcasted_iota(jnp.int32, (1, wv), 1)
    first_map = jnp.where(lane < DIFF_HEAD_DIM, 1.0, 0.0).astype(BF16)
    second_map = jnp.where(lane >= DIFF_HEAD_DIM, 1.0, 0.0).astype(BF16)
    for hd in heads:
        q = q_ref[:, hd * wv:(hd + 1) * wv]
        qq_ref[hd, 0:tq, :] = q * first_map
        qq_ref[hd, tq:2 * tq, :] = q * second_map

    m_ref[...] = jnp.full(m_ref.shape, -jnp.inf, F32)
    l_ref[...] = jnp.zeros(l_ref.shape, F32)
    acc_ref[...] = jnp.zeros(acc_ref.shape, F32)

    def scores(j, hd):
        k_t = k_ref[pl.ds(pl.multiple_of(j * tk, tk), tk), hd * wv:(hd + 1) * wv]
        s_ref[hd] = lax.dot_general(k_t, qq_ref[hd], (((1,), (1,)), ((), ())),
                                    preferred_element_type=F32)

    def accumulate(j, hd, key_offset=None):
        s = s_ref[hd]
        if key_offset is not None:
            key = lax.broadcasted_iota(jnp.int32, (tk, tq), 0) + key_offset
            qry = lax.broadcasted_iota(jnp.int32, (tk, tq), 1)
            ok = key <= qry
            s = jnp.where(jnp.concatenate([ok, ok], axis=1), s, -jnp.inf)
        m_old = m_ref[hd]
        m_new = jnp.maximum(m_old, jnp.max(s, axis=0, keepdims=True))
        alpha = jnp.exp2(m_old - m_new)
        e = jnp.exp2(s - m_new)
        l_ref[hd] = alpha * l_ref[hd] + jnp.sum(e, axis=0, keepdims=True)
        acc_ref[hd] = alpha * acc_ref[hd] + jnp.dot(vt_ref[hd, j], e.astype(BF16), preferred_element_type=F32)
        m_ref[hd] = m_new

    for hd in heads:
        scores(0, hd)

    def body(j, carry):
        for hd in heads:
            accumulate(j, hd)
            scores(j + 1, hd)
        return carry

    diag_tiles = tq // tk
    first_diag = i * diag_tiles
    lax.fori_loop(0, first_diag, body, 0)
    for t in range(diag_tiles):
        for hd in heads:
            accumulate(first_diag + t, hd, key_offset=t * tk)
            if t + 1 < diag_tiles:
                scores(first_diag + t + 1, hd)

    lam_init = cst_ref[0:1, 0:1]
    out_scale = cst_ref[0:1, 1:2]
    lam = (jnp.exp(jnp.sum(lq1_ref[...] * lk1_ref[...], axis=-1, keepdims=True))
           - jnp.exp(jnp.sum(lq2_ref[...] * lk2_ref[...], axis=-1, keepdims=True)) + lam_init)
    for hd in heads:
        on = acc_ref[hd] * (1.0 / l_ref[hd])
        ot = on[:, :tq] - lam * on[:, tq:]
        inv = lax.rsqrt(jnp.mean(ot * ot, axis=0, keepdims=True) + 1e-5) * out_scale
        o_ref[:, hd * wv:(hd + 1) * wv] = (ot * inv * subln_ref[...]).T.astype(o_ref.dtype)


def _diff_attn(ub3, lq1, lk1, lq2, lk2, cst, subln, layer):
    bn, s_len, _ = ub3.shape
    tq, hp = ATT_TQ, ATT_HEADS_PER_STEP
    assert s_len % tq == 0 and DIFF_HEADS % hp == 0
    nq = s_len // tq
    wv = DIFF_VAL_DIM
    wb = hp * wv
    vec = lambda width: pl.BlockSpec((None, 1, width), lambda b, h, i: (layer, 0, 0))
    return pl.pallas_call(
        _diff_attn_kernel,
        grid=(bn, DIFF_HEADS // hp, nq),
        in_specs=[
            vec(DIFF_HEAD_DIM), vec(DIFF_HEAD_DIM), vec(DIFF_HEAD_DIM), vec(DIFF_HEAD_DIM),
            vec(128),
            pl.BlockSpec((None, wv, tq), lambda b, h, i: (layer, 0, 0)),
            pl.BlockSpec((None, tq, wb), lambda b, h, i: (b, i, COLB_DQ // wb + h)),
            pl.BlockSpec((None, s_len, wb), lambda b, h, i: (b, 0, COLB_DK // wb + h)),
            pl.BlockSpec((None, s_len, wb), lambda b, h, i: (b, 0, COLB_DV // wb + h)),
        ],
        out_specs=pl.BlockSpec((None, tq, wb), lambda b, h, i: (b, i, h)),
        out_shape=jax.ShapeDtypeStruct((bn, s_len, DIFF_HEADS * wv), BF16),
        scratch_shapes=[
            pltpu.VMEM((hp, 2 * tq, wv), BF16),
            pltpu.VMEM((hp, s_len // ATT_TK, wv, ATT_TK), BF16),
            pltpu.VMEM((hp, ATT_TK, 2 * tq), F32),
            pltpu.VMEM((hp, 1, 2 * tq), F32),
            pltpu.VMEM((hp, 1, 2 * tq), F32),
            pltpu.VMEM((hp, wv, 2 * tq), F32),
        ],
        compiler_params=_params("parallel", "parallel", "arbitrary"),
        name="diff_attn",
    )(lq1, lk1, lq2, lk2, cst, subln, ub3, ub3, ub3)


def _retention_kernel(lg_ref, cos_ref, sin_ref, q_ref, k_ref, v_ref, g_ref, o_ref, state_ref):
    c = RET_CHUNK
    dk, dv = RET_KEY_DIM, RET_VAL_DIM
    n_chunks = q_ref.shape[0] // c
    heads = range(RET_HEADS_PER_STEP)
    row = lax.broadcasted_iota(jnp.int32, (c, c), 0).astype(F32)
    col = lax.broadcasted_iota(jnp.int32, (c, c), 1).astype(F32)
    rel = row - col
    row_v = lax.broadcasted_iota(jnp.int32, (c, dv), 0).astype(F32)
    tables = []
    for hd in heads:
        lg = lg_ref[hd]
        decay = jnp.where(rel >= 0, jnp.exp(jnp.maximum(rel, 0.0) * lg[:, :c]), 0.0)
        w_k = jnp.exp((c - 1.0 - row) * lg[:, :c])
        w_q = jnp.exp((row_v + 1.0) * lg)
        chunk_decay = jnp.exp(float(c) * lg)
        tables.append((decay, w_k, w_q, chunk_decay))

    def rotary(t, cos, sin):
        return t * cos + pltpu.roll(t, dk // 2, axis=1) * sin

    state_ref[...] = jnp.zeros(state_ref.shape, F32)

    def body(nn, carry):
        work = {}
        for t in range(RET_CHUNKS_PER_TRIP):
            for hd in heads:
                _, w_k, _, _ = tables[hd]
                rows = pl.ds(pl.multiple_of((nn * RET_CHUNKS_PER_TRIP + t) * c, c), c)
                kcols = slice(hd * dk, (hd + 1) * dk)
                vcols = slice(hd * dv, (hd + 1) * dv)
                cos, sin = cos_ref[rows, :], sin_ref[rows, :]
                qb = rotary(q_ref[rows, kcols], cos, sin).astype(BF16)
                kr = rotary(k_ref[rows, kcols], cos, sin) * (dk ** -0.5)
                vb = v_ref[rows, vcols]
                s = lax.dot_general(qb, kr.astype(BF16), (((1,), (1,)), ((), ())), preferred_element_type=F32)
                kv = jnp.dot((kr * w_k).T.astype(BF16), vb, preferred_element_type=F32)
                work[t, hd] = (rows, vcols, qb, vb, s, kv)
        outs = {}
        for hd in heads:
            decay, _, w_q, chunk_decay = tables[hd]
            state = state_ref[hd]
            for t in range(RET_CHUNKS_PER_TRIP):
                rows, vcols, qb, vb, s, kv = work[t, hd]
                outs[t, hd] = (jnp.dot((s * decay).astype(BF16), vb, preferred_element_type=F32)
                               + jnp.dot(qb, state.astype(BF16), preferred_element_type=F32) * w_q)
                state = chunk_decay * state + kv
            state_ref[hd] = state
        for (t, hd), o in outs.items():
            rows, vcols = work[t, hd][:2]
            mu = jnp.mean(o, axis=-1, keepdims=True)
            var = jnp.mean(jnp.square(o - mu), axis=-1, keepdims=True)
            on = (o - mu) * lax.rsqrt(var + 1e-5)
            o_ref[rows, vcols] = (g_ref[rows, vcols].astype(F32) * on).astype(o_ref.dtype)
        return carry

    lax.fori_loop(0, n_chunks // RET_CHUNKS_PER_TRIP, body, 0)


def _retention(ub3, uf3, lg, cos, sin):
    bn, s_len, _ = uf3.shape
    dk, dv = RET_KEY_DIM, RET_VAL_DIM
    hp = RET_HEADS_PER_STEP
    assert s_len % (RET_CHUNK * RET_CHUNKS_PER_TRIP) == 0 and RET_HEADS % hp == 0
    wk, wv = hp * dk, hp * dv
    return pl.pallas_call(
        _retention_kernel,
        grid=(bn, RET_HEADS // hp),
        in_specs=[
            pl.BlockSpec((hp, 1, dv), lambda b, h: (h, 0, 0)),
            pl.BlockSpec((s_len, dk), lambda b, h: (0, 0)),
            pl.BlockSpec((s_len, dk), lambda b, h: (0, 0)),
            pl.BlockSpec((None, s_len, wk), lambda b, h: (b, 0, COLF_RQ // wk + h)),
            pl.BlockSpec((None, s_len, wk), lambda b, h: (b, 0, COLF_RK // wk + h)),
            pl.BlockSpec((None, s_len, wv), lambda b, h: (b, 0, COLB_RV // wv + h)),
            pl.BlockSpec((None, s_len, wv), lambda b, h: (b, 0, COLB_RG // wv + h)),
        ],
        out_specs=pl.BlockSpec((None, s_len, wv), lambda b, h: (b, 0, h)),
        out_shape=jax.ShapeDtypeStruct((bn, s_len, RET_HEADS * dv), BF16),
        scratch_shapes=[pltpu.VMEM((hp, dk, dv), F32)],
        compiler_params=_params("parallel", "parallel"),
        name="retention",
    )(lg, cos, sin, uf3, uf3, ub3, ub3)


def _softplus(z):
    return jnp.maximum(z, 0.0) + jnp.log1p(jnp.exp(-jnp.abs(z)))


def _rglru_kernel(cw_ref, cb_ref, wa_ref, ba_ref, wx_ref, bx_ref, lam_ref, x_ref, y_ref, o_ref,
                  xpad_ref, hs_ref, h_ref):
    ts = x_ref.shape[0]
    halo = SUBLANES
    groups = ts // SUBLANES

    @pl.when(pl.program_id(1) == 0)
    def _():
        xpad_ref[:, 0:halo, :] = jnp.zeros((LRU_BLOCKS, halo, LRU_BLOCK_W), F32)
        h_ref[...] = jnp.zeros(h_ref.shape, F32)

    @pl.when(pl.program_id(1) > 0)
    def _():
        xpad_ref[:, 0:halo, :] = xpad_ref[:, ts:ts + halo, :]

    row = lax.broadcasted_iota(jnp.int32, (groups, LRU_BLOCK_W), 0)

    for n in range(LRU_BLOCKS):
        cols = slice(n * LRU_BLOCK_W, (n + 1) * LRU_BLOCK_W)
        xpad_ref[n, halo:halo + ts, :] = x_ref[:, cols]
        phase = {p: xpad_ref[n, pl.ds(halo + p, groups, stride=SUBLANES), :]
                 for p in range(1 - CONV_WIDTH, SUBLANES)}
        xcs = []
        for p in range(SUBLANES):
            xc = cb_ref[:, cols]
            for j in range(CONV_WIDTH):
                xc = xc + cw_ref[j:j + 1, cols] * phase[p + j - (CONV_WIDTH - 1)]
            xcs.append(xc)
        xc = jnp.concatenate(xcs, axis=0)
        xb = xc.astype(BF16)
        r = _sigmoid(jnp.dot(xb, wa_ref[n], preferred_element_type=F32) + ba_ref[:, cols])
        gi = _sigmoid(jnp.dot(xb, wx_ref[n], preferred_element_type=F32) + bx_ref[:, cols])
        log_a = (-LRU_C * _softplus(-lam_ref[:, cols])) * r
        a = jnp.exp(log_a)
        b = jnp.sqrt(1.0 - a * a) * (gi * xc)

        a_run, b_run = a[0:groups], b[0:groups]
        a_runs, b_runs = [a_run], [b_run]
        for p in range(1, SUBLANES):
            a_p = a[p * groups:(p + 1) * groups]
            b_run = a_p * b_run + b[p * groups:(p + 1) * groups]
            a_run = a_p * a_run
            a_runs.append(a_run)
            b_runs.append(b_run)
        h0 = h_ref[:, cols]
        end = jnp.where(row == 0, b_run + a_run * h0, b_run)
        prod = a_run
        d = 1
        while d < groups:
            keep = row >= d
            end = jnp.where(keep, prod * pltpu.roll(end, d, axis=0) + end, end)
            prod = jnp.where(keep, prod * pltpu.roll(prod, d, axis=0), prod)
            d *= 2
        h_in = jnp.where(row == 0, h0, pltpu.roll(end, 1, axis=0))
        h_ref[:, cols] = end[groups - 1:groups, :]
        for p in range(SUBLANES):
            hs_ref[n, pl.ds(p, groups, stride=SUBLANES), :] = b_runs[p] + a_runs[p] * h_in
        o_ref[:, cols] = (hs_ref[n] * y_ref[:, cols].astype(F32)).astype(o_ref.dtype)


def _rglru(ub3, uf3, conv_w, conv_b, wa, ba, wx, bx, lam, layer):
    bn, s_len, _ = uf3.shape
    w = LRU_BLOCKS * LRU_BLOCK_W
    ts = min(LRU_TS, s_len)
    vec = lambda: pl.BlockSpec((None, 1, w), lambda b, t: (layer, 0, 0))
    blk = lambda: pl.BlockSpec((None, LRU_BLOCKS, LRU_BLOCK_W, LRU_BLOCK_W), lambda b, t: (layer, 0, 0, 0))
    return pl.pallas_call(
        _rglru_kernel,
        grid=(bn, s_len // ts),
        in_specs=[
            pl.BlockSpec((None, CONV_WIDTH, w), lambda b, t: (layer, 0, 0)),
            vec(), blk(), vec(), blk(), vec(), vec(),
            pl.BlockSpec((None, ts, w), lambda b, t: (b, t, COLF_LX // w)),
            pl.BlockSpec((None, ts, w), lambda b, t: (b, t, COLB_LY // w)),
        ],
        out_specs=pl.BlockSpec((None, ts, w), lambda b, t: (b, t, 0)),
        out_shape=jax.ShapeDtypeStruct((bn, s_len, w), BF16),
        scratch_shapes=[pltpu.VMEM((LRU_BLOCKS, ts + SUBLANES, LRU_BLOCK_W), F32),
                        pltpu.VMEM((LRU_BLOCKS, ts, LRU_BLOCK_W), F32), pltpu.VMEM((1, w), F32)],
        compiler_params=_params("parallel", "arbitrary"),
        name="rglru",
    )(conv_w, conv_b, wa, ba, wx, bx, lam, uf3, ub3)


def _merge_kernel(od_ref, or_ref, ol_ref, g0_ref, g1_ref, g2_ref, x_ref, wb_ref, wo_ref, o_ref):
    merged = (g0_ref[...].astype(F32) * jnp.dot(od_ref[...], wb_ref[0], preferred_element_type=F32)
              + g1_ref[...].astype(F32) * jnp.dot(or_ref[...], wb_ref[1], preferred_element_type=F32)
              + g2_ref[...].astype(F32) * jnp.dot(ol_ref[...], wb_ref[2], preferred_element_type=F32))
    o_ref[...] = x_ref[...] + jnp.dot(merged.astype(BF16), wo_ref[...], preferred_element_type=F32)


def _merge(od, o_r, o_l, u2, x2, w_branch, w_out, layer, tm=512):
    m, d = x2.shape
    tm = min(tm, m)
    row = lambda: pl.BlockSpec((tm, d), lambda i: (i, 0))
    gate = lambda k: pl.BlockSpec((tm, d), lambda i: (i, COLB_GATES // d + k))
    return pl.pallas_call(
        _merge_kernel,
        grid=(m // tm,),
        in_specs=[
            row(), row(), row(), gate(0), gate(1), gate(2), row(),
            pl.BlockSpec((None, 3, d, d), lambda i: (layer, 0, 0, 0)),
            pl.BlockSpec((None, d, d), lambda i: (layer, 0, 0)),
        ],
        out_specs=row(),
        out_shape=jax.ShapeDtypeStruct((m, d), F32),
        compiler_params=_params("parallel"),
        name="merge",
    )(od, o_r, o_l, u2, u2, u2, x2, w_branch, w_out)


def _xattn_kernel(x_ref, g_ref, wq_ref, kv_ref, wo_ref, o_ref):
    x = x_ref[...]
    h = _rms_rows(x, g_ref[...], 1e-6).astype(BF16)
    q = jnp.dot(h, wq_ref[...], preferred_element_type=F32).astype(BF16)
    heads = []
    for hd in range(XA_HEADS):
        kc = slice(hd * XA_HEAD_DIM, (hd + 1) * XA_HEAD_DIM)
        vc = slice(XA_WIDTH + hd * XA_HEAD_DIM, XA_WIDTH + (hd + 1) * XA_HEAD_DIM)
        s = lax.dot_general(q[:, kc], kv_ref[:, kc], (((1,), (1,)), ((), ())),
                            preferred_element_type=F32) * (XA_HEAD_DIM ** -0.5)
        e = jnp.exp(s - jnp.max(s, axis=-1, keepdims=True))
        pv = jnp.dot(e.astype(BF16), kv_ref[:, vc], preferred_element_type=F32)
        heads.append((pv / jnp.sum(e, axis=-1, keepdims=True)).astype(BF16))
    o = jnp.concatenate(heads, axis=-1)
    o_ref[...] = x + jnp.dot(o, wo_ref[...], preferred_element_type=F32)


def _xattn(x3, g, wq, kv3, wo, layer, tm=1024):
    bn, s_len, d = x3.shape
    tm = min(tm, s_len)
    m_len = kv3.shape[1]
    return pl.pallas_call(
        _xattn_kernel,
        grid=(bn, s_len // tm),
        in_specs=[
            pl.BlockSpec((None, tm, d), lambda b, i: (b, i, 0)),
            pl.BlockSpec((None, 1, d), lambda b, i: (layer, 0, 0)),
            pl.BlockSpec((None, d, XA_WIDTH), lambda b, i: (layer, 0, 0)),
            pl.BlockSpec((None, m_len, 2 * XA_WIDTH), lambda b, i: (b, 0, 0)),
            pl.BlockSpec((None, XA_WIDTH, d), lambda b, i: (layer, 0, 0)),
        ],
        out_specs=pl.BlockSpec((None, tm, d), lambda b, i: (b, i, 0)),
        out_shape=jax.ShapeDtypeStruct((bn, s_len, d), F32),
        compiler_params=_params("parallel", "parallel"),
        name="xattn",
    )(x3, g, wq, kv3, wo)


def _mlp_kernel(x_ref, g_ref, w1_ref, w2_ref, gf_ref, o_ref, *, final_norm):
    x = x_ref[...]
    h = _rms_rows(x, g_ref[...], 1e-6).astype(BF16)
    d = x.shape[1]
    acc = x
    for c in range(w1_ref.shape[1] // d):
        cols = slice(c * d, (c + 1) * d)
        a = jnp.maximum(jnp.dot(h, w1_ref[:, cols], preferred_element_type=F32), 0.0)
        acc = acc + jnp.dot((a * a).astype(BF16), w2_ref[cols, :], preferred_element_type=F32)
    o_ref[...] = _rms_rows(acc, gf_ref[...], 1e-6) if final_norm else acc


def _mlp(x2, g, w1, w2, g_final, layer, final_norm, tm=512):
    m, d = x2.shape
    tm = min(tm, m)
    dff = w1.shape[2]
    return pl.pallas_call(
        functools.partial(_mlp_kernel, final_norm=final_norm),
        grid=(m // tm,),
        in_specs=[
            pl.BlockSpec((tm, d), lambda i: (i, 0)),
            pl.BlockSpec((None, 1, d), lambda i: (layer, 0, 0)),
            pl.BlockSpec((None, d, dff), lambda i: (layer, 0, 0)),
            pl.BlockSpec((None, dff, d), lambda i: (layer, 0, 0)),
            pl.BlockSpec((1, d), lambda i: (0, 0)),
        ],
        out_specs=pl.BlockSpec((tm, d), lambda i: (i, 0)),
        out_shape=jax.ShapeDtypeStruct((m, d), F32),
        compiler_params=_params("parallel"),
        name="mlp",
    )(x2, g, w1, w2, g_final)


def kernel(x, mem, norm_mix, w_in, diff_lq1, diff_lk1, diff_lq2, diff_lk2, diff_subln, lru_conv_w, lru_conv_b, lru_wa, lru_ba, lru_wx, lru_bx, lru_lambda, w_branch, w_out, norm_xattn, norm_mem, xa_wq, xa_wkv, xa_wo, norm_mlp, mlp_w1, mlp_w2, norm_final):
    bn, s_len, d = x.shape
    m_len = mem.shape[1]
    depth = norm_mix.shape[0]
    rows = bn * s_len

    qk0, qk1 = 3 * PROJ_TN, 4 * PROJ_TN
    w_qk = w_in[:, :, qk0:qk1].reshape(depth, d, (qk1 - qk0) // RET_KEY_DIM, RET_KEY_DIM // 2, 2)
    w_qk = jnp.swapaxes(w_qk, -1, -2).reshape(depth, d, qk1 - qk0)
    w_in_b = lax.dynamic_update_slice(w_in.astype(BF16), w_qk.astype(BF16), (0, 0, qk0))
    w_branch_b, w_out_b = w_branch.astype(BF16), w_out.astype(BF16)
    wq_b, wkv_b, wo_b = xa_wq.astype(BF16), xa_wkv.astype(BF16), xa_wo.astype(BF16)
    w1_b, w2_b = mlp_w1.astype(BF16), mlp_w2.astype(BF16)
    wa_b, wx_b = lru_wa.astype(BF16), lru_wx.astype(BF16)
    row3 = lambda p: p.reshape(depth, 1, p.shape[-1])
    subln_lanes = jnp.broadcast_to(diff_subln[:, :, None], (depth, DIFF_VAL_DIM, ATT_TQ))
    conv_w = lru_conv_w.reshape(depth, CONV_WIDTH, d)

    pos = jnp.arange(s_len, dtype=F32)
    angle = 1.0 / (10000.0 ** jnp.linspace(0.0, 1.0, RET_KEY_DIM // 2, dtype=F32))
    phase = pos[:, None] * angle[None, :]
    cos = jnp.concatenate([jnp.cos(phase), jnp.cos(phase)], axis=1)
    sin = jnp.concatenate([-jnp.sin(phase), jnp.sin(phase)], axis=1)
    log_g = jnp.log(1.0 - jnp.exp2(-5.0 - jnp.arange(RET_HEADS, dtype=F32)))
    lg = jnp.broadcast_to(log_g[:, None, None], (RET_HEADS, 1, RET_VAL_DIM))
    lam_inits = [0.8 - 0.6 * math.exp(-0.3 * l) for l in range(depth)]
    cst = jnp.zeros((depth, 1, 128), F32)
    cst = cst.at[:, 0, 0].set(jnp.asarray(lam_inits, F32))
    cst = cst.at[:, 0, 1].set(jnp.asarray([1.0 - v for v in lam_inits], F32))

    x2 = x.reshape(rows, d)
    mem2 = mem.reshape(bn * m_len, d)
    for l in range(depth):
        ub2, uf2 = _proj_in(x2, row3(norm_mix), w_in_b, l)
        ub3 = ub2.reshape(bn, s_len, ub2.shape[1])
        uf3 = uf2.reshape(bn, s_len, uf2.shape[1])
        od = _diff_attn(ub3, row3(diff_lq1), row3(diff_lk1), row3(diff_lq2), row3(diff_lk2), cst,
                        subln_lanes, l)
        o_r = _retention(ub3, uf3, lg, cos, sin)
        o_l = _rglru(ub3, uf3, conv_w, row3(lru_conv_b), wa_b, row3(lru_ba), wx_b, row3(lru_bx),
                     row3(lru_lambda), l)
        x2 = _merge(od.reshape(rows, d), o_r.reshape(rows, d), o_l.reshape(rows, d), ub2, x2,
                    w_branch_b, w_out_b, l)
        kv = _norm_matmul(mem2, row3(norm_mem), wkv_b, l, 2 * XA_WIDTH, BF16, tm=512, tn=2 * XA_WIDTH)
        x2 = _xattn(x2.reshape(bn, s_len, d), row3(norm_xattn), wq_b, kv.reshape(bn, m_len, 2 * XA_WIDTH),
                    wo_b, l).reshape(rows, d)
        x2 = _mlp(x2, row3(norm_mlp), w1_b, w2_b, norm_final.reshape(1, d), l, final_norm=(l == depth - 1))
    return x2.reshape(bn, s_len, d)
```

```python
import functools
import math

import jax
import jax.numpy as jnp
from jax import lax
from jax.experimental import pallas as pl
from jax.experimental.pallas import tpu as pltpu

F32 = jnp.float32
BF16 = jnp.bfloat16

D_MODEL = 1024
DIFF_HEADS = 8
DIFF_HEAD_DIM = 64
DIFF_VAL_DIM = 2 * DIFF_HEAD_DIM
RET_HEADS = 4
RET_KEY_DIM = 128
RET_VAL_DIM = 256
RET_CHUNK = 128
LRU_BLOCKS = 8
LRU_BLOCK_W = 128
CONV_WIDTH = 4
LRU_C = 8.0
XA_HEADS = 4
XA_HEAD_DIM = 128
XA_WIDTH = XA_HEADS * XA_HEAD_DIM
D_FF = 4 * D_MODEL

PROJ_TN = 1024
PROJ_CHUNK = 256
PROJ_TILES = (("dq", "qscale"), ("dk", "b"), ("dv", "b"), ("rqk", "f"), ("rv", "b"), ("rg", "silu"),
              ("lx", "f"), ("ly", "gelu"), ("g0", "sigmoid"), ("g1", "sigmoid"), ("g2", "sigmoid"))
PROJ_KINDS = tuple(kind for _, kind in PROJ_TILES)


def _proj_col(name):
    names = [n for n, _ in PROJ_TILES]
    kind = PROJ_KINDS[names.index(name)]
    same = [n for n, k in PROJ_TILES if (k == "f") == (kind == "f")]
    return same.index(name) * PROJ_TN


COLB_DQ, COLB_DK, COLB_DV, COLB_RV = (_proj_col(n) for n in ("dq", "dk", "dv", "rv"))
COLB_RG, COLB_LY, COLB_GATES = (_proj_col(n) for n in ("rg", "ly", "g0"))
COLF_RQ, COLF_LX = _proj_col("rqk"), _proj_col("lx")
COLF_RK = COLF_RQ + RET_HEADS * RET_KEY_DIM

VMEM_LIMIT_BYTES = 56 * 1024 * 1024

ATT_TQ = 512
ATT_TK = 512
ATT_HEADS_PER_STEP = 8
RET_CHUNKS_PER_TRIP = 4
RET_HEADS_PER_STEP = 2
LRU_TS = 512
SUBLANES = 8


def _params(*sem):
    return pltpu.CompilerParams(dimension_semantics=sem, vmem_limit_bytes=VMEM_LIMIT_BYTES)


def _rms_rows(xf, g, eps):
    return xf * lax.rsqrt(jnp.mean(xf * xf, axis=-1, keepdims=True) + eps) * g


def _norm_matmul_kernel(x_ref, g_ref, w_ref, o_ref, h_ref):
    @pl.when(pl.program_id(1) == 0)
    def _():
        h_ref[...] = _rms_rows(x_ref[...], g_ref[...], 1e-6).astype(BF16)

    o_ref[...] = jnp.dot(h_ref[...], w_ref[...], preferred_element_type=F32).astype(o_ref.dtype)


def _norm_matmul(x2, g, w, layer, n_cols, out_dtype, tm, tn):
    m, d = x2.shape
    tm = min(tm, m)
    return pl.pallas_call(
        _norm_matmul_kernel,
        grid=(m // tm, n_cols // tn),
        in_specs=[
            pl.BlockSpec((tm, d), lambda i, j: (i, 0)),
            pl.BlockSpec((None, 1, d), lambda i, j: (layer, 0, 0)),
            pl.BlockSpec((None, d, tn), lambda i, j: (layer, 0, j)),
        ],
        out_specs=pl.BlockSpec((tm, tn), lambda i, j: (i, j)),
        out_shape=jax.ShapeDtypeStruct((m, n_cols), out_dtype),
        scratch_shapes=[pltpu.VMEM((tm, d), BF16)],
        compiler_params=_params("parallel", "arbitrary"),
        name="norm_matmul",
    )(x2, g, w)


def _held_index(is_f32):
    table, n = [], 0
    for kind in PROJ_KINDS:
        if (kind == "f") == is_f32:
            n += 1
        table.append(max(n - 1, 0))
    steps = [j for j in range(1, len(table)) if table[j] != table[j - 1]]
    return lambda j: sum((j >= s).astype(jnp.int32) for s in steps)


def _gelu_tanh(y):
    c = math.sqrt(2.0 / math.pi)
    half = 0.5 * y
    return half + half * jnp.tanh(y * (c + (c * 0.044715) * (y * y)))


def _sigmoid(z):
    return 0.5 * jnp.tanh(0.5 * z) + 0.5


def _silu(z):
    half = 0.5 * z
    return half + half * jnp.tanh(half)


_PROJ_EPILOGUES = {
    "f": lambda u: u,
    "b": lambda u: u,
    "qscale": lambda u: u * (DIFF_HEAD_DIM ** -0.5 * math.log2(math.e)),
    "silu": _silu,
    "gelu": _gelu_tanh,
    "sigmoid": _sigmoid,
}


def _proj_in_kernel(x_ref, g_ref, w_ref, ob_ref, of_ref, h_ref):
    j = pl.program_id(1)

    @pl.when(j == 0)
    def _():
        h_ref[...] = _rms_rows(x_ref[...], g_ref[...], 1e-6).astype(BF16)

    for kind in sorted(set(PROJ_KINDS)):
        is_kind = functools.reduce(jnp.logical_or, [j == t for t, k in enumerate(PROJ_KINDS) if k == kind])
        out_ref = of_ref if kind == "f" else ob_ref

        @pl.when(is_kind)
        def _(kind=kind, out_ref=out_ref):
            for r in range(0, h_ref.shape[0], PROJ_CHUNK):
                u = jnp.dot(h_ref[r:r + PROJ_CHUNK, :], w_ref[...], preferred_element_type=F32)
                out_ref[r:r + PROJ_CHUNK, :] = _PROJ_EPILOGUES[kind](u).astype(out_ref.dtype)


def _proj_in(x2, g, w, layer, tm=2048):
    m, d = x2.shape
    tm = min(tm, m)
    tn = PROJ_TN
    idx_b, idx_f = _held_index(False), _held_index(True)
    n_f32 = PROJ_KINDS.count("f")
    return pl.pallas_call(
        _proj_in_kernel,
        grid=(m // tm, len(PROJ_KINDS)),
        in_specs=[
            pl.BlockSpec((tm, d), lambda i, j: (i, 0)),
            pl.BlockSpec((None, 1, d), lambda i, j: (layer, 0, 0)),
            pl.BlockSpec((None, d, tn), lambda i, j: (layer, 0, j)),
        ],
        out_specs=[
            pl.BlockSpec((tm, tn), lambda i, j: (i, idx_b(j))),
            pl.BlockSpec((tm, tn), lambda i, j: (i, idx_f(j))),
        ],
        out_shape=[
            jax.ShapeDtypeStruct((m, tn * (len(PROJ_KINDS) - n_f32)), BF16),
            jax.ShapeDtypeStruct((m, tn * n_f32), F32),
        ],
        scratch_shapes=[pltpu.VMEM((tm, d), BF16)],
        compiler_params=_params("parallel", "arbitrary"),
        name="proj_in",
    )(x2, g, w)


def _diff_attn_kernel(lq1_ref, lk1_ref, lq2_ref, lk2_ref, cst_ref, subln_ref, q_ref, k_ref, v_ref,
                      o_ref, qq_ref, vt_ref, s_ref, m_ref, l_ref, acc_ref):
    i = pl.program_id(2)
    tq, tk, wv = ATT_TQ, ATT_TK, DIFF_VAL_DIM
    n_kv = k_ref.shape[0] // tk
    heads = range(ATT_HEADS_PER_STEP)

    @pl.when(i == 0)
    def _():
        for hd in heads:
            for t in range(n_kv):
                v_t = v_ref[t * tk:(t + 1) * tk, hd * wv:(hd + 1) * wv]
                vt_ref[hd, t] = v_t.astype(F32).T.astype(BF16)

    lane = lax.broadcasted_iota(jnp.int32, (1, wv), 1)
    first_map = jnp.where(lane < DIFF_HEAD_DIM, 1.0, 0.0).astype(BF16)
    second_map = jnp.where(lane >= DIFF_HEAD_DIM, 1.0, 0.0).astype(BF16)
    for hd in heads:
        q = q_ref[:, hd * wv:(hd + 1) * wv]
        qq_ref[hd, 0:tq, :] = q * first_map
        qq_ref[hd, tq:2 * tq, :] = q * second_map

    m_ref[...] = jnp.full(m_ref.shape, -jnp.inf, F32)
    l_ref[...] = jnp.zeros(l_ref.shape, F32)
    acc_ref[...] = jnp.zeros(acc_ref.shape, F32)

    def scores(j, hd):
        k_t = k_ref[pl.ds(pl.multiple_of(j * tk, tk), tk), hd * wv:(hd + 1) * wv]
        s_ref[hd] = lax.dot_general(k_t, qq_ref[hd], (((1,), (1,)), ((), ())),
                                    preferred_element_type=F32)

    def accumulate(j, hd, key_offset=None):
        s = s_ref[hd]
        if key_offset is not None:
            key = lax.broadcasted_iota(jnp.int32, (tk, tq), 0) + key_offset
            qry = lax.broadcasted_iota(jnp.int32, (tk, tq), 1)
            ok = key <= qry
            s = jnp.where(jnp.concatenate([ok, ok], axis=1), s, -jnp.inf)
        m_old = m_ref[hd]
        m_new = jnp.maximum(m_old, jnp.max(s, axis=0, keepdims=True))
        alpha = jnp.exp2(m_old - m_new)
        e = jnp.exp2(s - m_new)
        l_ref[hd] = alpha * l_ref[hd] + jnp.sum(e, axis=0, keepdims=True)
        acc_ref[hd] = alpha * acc_ref[hd] + jnp.dot(vt_ref[hd, j], e.astype(BF16), preferred_element_type=F32)
        m_ref[hd] = m_new

    for hd in heads:
        scores(0, hd)

    def body(j, carry):
        for hd in heads:
            accumulate(j, hd)
            scores(j + 1, hd)
        return carry

    diag_tiles = tq // tk
    first_diag = i * diag_tiles
    lax.fori_loop(0, first_diag, body, 0)
    for t in range(diag_tiles):
        for hd in heads:
            accumulate(first_diag + t, hd, key_offset=t * tk)
            if t + 1 < diag_tiles:
                scores(first_diag + t + 1, hd)

    lam_init = cst_ref[0:1, 0:1]
    out_scale = cst_ref[0:1, 1:2]
    lam = (jnp.exp(jnp.sum(lq1_ref[...] * lk1_ref[...], axis=-1, keepdims=True))
           - jnp.exp(jnp.sum(lq2_ref[...] * lk2_ref[...], axis=-1, keepdims=True)) + lam_init)
    for hd in heads:
        on = acc_ref[hd] * (1.0 / l_ref[hd])
        ot = on[:, :tq] - lam * on[:, tq:]
        inv = lax.rsqrt(jnp.mean(ot * ot, axis=0, keepdims=True) + 1e-5) * out_scale
        o_ref[:, hd * wv:(hd + 1) * wv] = (ot * inv * subln_ref[...]).T.astype(o_ref.dtype)


def _diff_attn(ub3, lq1, lk1, lq2, lk2, cst, subln, layer):
    bn, s_len, _ = ub3.shape
    tq, hp = ATT_TQ, ATT_HEADS_PER_STEP
    assert s_len % tq == 0 and DIFF_HEADS % hp == 0
    nq = s_len // tq
    wv = DIFF_VAL_DIM
    wb = hp * wv
    vec = lambda width: pl.BlockSpec((None, 1, width), lambda b, h, i: (layer, 0, 0))
    return pl.pallas_call(
        _diff_attn_kernel,
        grid=(bn, DIFF_HEADS // hp, nq),
        in_specs=[
            vec(DIFF_HEAD_DIM), vec(DIFF_HEAD_DIM), vec(DIFF_HEAD_DIM), vec(DIFF_HEAD_DIM),
            vec(128),
            pl.BlockSpec((None, wv, tq), lambda b, h, i: (layer, 0, 0)),
            pl.BlockSpec((None, tq, wb), lambda b, h, i: (b, i, COLB_DQ // wb + h)),
            pl.BlockSpec((None, s_len, wb), lambda b, h, i: (b, 0, COLB_DK // wb + h)),
            pl.BlockSpec((None, s_len, wb), lambda b, h, i: (b, 0, COLB_DV // wb + h)),
        ],
        out_specs=pl.BlockSpec((None, tq, wb), lambda b, h, i: (b, i, h)),
        out_shape=jax.ShapeDtypeStruct((bn, s_len, DIFF_HEADS * wv), BF16),
        scratch_shapes=[
            pltpu.VMEM((hp, 2 * tq, wv), BF16),
            pltpu.VMEM((hp, s_len // ATT_TK, wv, ATT_TK), BF16),
            pltpu.VMEM((hp, ATT_TK, 2 * tq), F32),
            pltpu.VMEM((hp, 1, 2 * tq), F32),
            pltpu.VMEM((hp, 1, 2 * tq), F32),
            pltpu.VMEM((hp, wv, 2 * tq), F32),
        ],
        compiler_params=_params("parallel", "parallel", "arbitrary"),
        name="diff_attn",
    )(lq1, lk1, lq2, lk2, cst, subln, ub3, ub3, ub3)


def _retention_kernel(lg_ref, cos_ref, sin_ref, q_ref, k_ref, v_ref, g_ref, o_ref, state_ref):
    c = RET_CHUNK
    dk, dv = RET_KEY_DIM, RET_VAL_DIM
    n_chunks = q_ref.shape[0] // c
    heads = range(RET_HEADS_PER_STEP)
    row = lax.broadcasted_iota(jnp.int32, (c, c), 0).astype(F32)
    col = lax.broadcasted_iota(jnp.int32, (c, c), 1).astype(F32)
    rel = row - col
    row_v = lax.broadcasted_iota(jnp.int32, (c, dv), 0).astype(F32)
    tables = []
    for hd in heads:
        lg = lg_ref[hd]
        decay = jnp.where(rel >= 0, jnp.exp(jnp.maximum(rel, 0.0) * lg[:, :c]), 0.0)
        w_k = jnp.exp((c - 1.0 - row) * lg[:, :c])
        w_q = jnp.exp((row_v + 1.0) * lg)
        chunk_decay = jnp.exp(float(c) * lg)
        tables.append((decay, w_k, w_q, chunk_decay))

    def rotary(t, cos, sin):
        return t * cos + pltpu.roll(t, dk // 2, axis=1) * sin

    state_ref[...] = jnp.zeros(state_ref.shape, F32)

    def body(nn, carry):
        work = {}
        for t in range(RET_CHUNKS_PER_TRIP):
            for hd in heads:
                _, w_k, _, _ = tables[hd]
                rows = pl.ds(pl.multiple_of((nn * RET_CHUNKS_PER_TRIP + t) * c, c), c)
                kcols = slice(hd * dk, (hd + 1) * dk)
                vcols = slice(hd * dv, (hd + 1) * dv)
                cos, sin = cos_ref[rows, :], sin_ref[rows, :]
                qb = rotary(q_ref[rows, kcols], cos, sin).astype(BF16)
                kr = rotary(k_ref[rows, kcols], cos, sin) * (dk ** -0.5)
                vb = v_ref[rows, vcols]
                s = lax.dot_general(qb, kr.astype(BF16), (((1,), (1,)), ((), ())), preferred_element_type=F32)
                kv = jnp.dot((kr * w_k).T.astype(BF16), vb, preferred_element_type=F32)
                work[t, hd] = (rows, vcols, qb, vb, s, kv)
        outs = {}
        for hd in heads:
            decay, _, w_q, chunk_decay = tables[hd]
            state = state_ref[hd]
            for t in range(RET_CHUNKS_PER_TRIP):
                rows, vcols, qb, vb, s, kv = work[t, hd]
                outs[t, hd] = (jnp.dot((s * decay).astype(BF16), vb, preferred_element_type=F32)
                               + jnp.dot(qb, state.astype(BF16), preferred_element_type=F32) * w_q)
                state = chunk_decay * state + kv
            state_ref[hd] = state
        for (t, hd), o in outs.items():
            rows, vcols = work[t, hd][:2]
            mu = jnp.mean(o, axis=-1, keepdims=True)
            var = jnp.mean(jnp.square(o - mu), axis=-1, keepdims=True)
            on = (o - mu) * lax.rsqrt(var + 1e-5)
            o_ref[rows, vcols] = (g_ref[rows, vcols].astype(F32) * on).astype(o_ref.dtype)
        return carry

    lax.fori_loop(0, n_chunks // RET_CHUNKS_PER_TRIP, body, 0)


def _retention(ub3, uf3, lg, cos, sin):
    bn, s_len, _ = uf3.shape
    dk, dv = RET_KEY_DIM, RET_VAL_DIM
    hp = RET_HEADS_PER_STEP
    assert s_len % (RET_CHUNK * RET_CHUNKS_PER_TRIP) == 0 and RET_HEADS % hp == 0
    wk, wv = hp * dk, hp * dv
    return pl.pallas_call(
        _retention_kernel,
        grid=(bn, RET_HEADS // hp),
        in_specs=[
            pl.BlockSpec((hp, 1, dv), lambda b, h: (h, 0, 0)),
            pl.BlockSpec((s_len, dk), lambda b, h: (0, 0)),
            pl.BlockSpec((s_len, dk), lambda b, h: (0, 0)),
            pl.BlockSpec((None, s_len, wk), lambda b, h: (b, 0, COLF_RQ // wk + h)),
            pl.BlockSpec((None, s_len, wk), lambda b, h: (b, 0, COLF_RK // wk + h)),
            pl.BlockSpec((None, s_len, wv), lambda b, h: (b, 0, COLB_RV // wv + h)),
            pl.BlockSpec((None, s_len, wv), lambda b, h: (b, 0, COLB_RG // wv + h)),
        ],
        out_specs=pl.BlockSpec((None, s_len, wv), lambda b, h: (b, 0, h)),
        out_shape=jax.ShapeDtypeStruct((bn, s_len, RET_HEADS * dv), BF16),
        scratch_shapes=[pltpu.VMEM((hp, dk, dv), F32)],
        compiler_params=_params("parallel", "parallel"),
        name="retention",
    )(lg, cos, sin, uf3, uf3, ub3, ub3)


def _softplus(z):
    return jnp.maximum(z, 0.0) + jnp.log1p(jnp.exp(-jnp.abs(z)))


def _rglru_kernel(cw_ref, cb_ref, wa_ref, ba_ref, wx_ref, bx_ref, lam_ref, x_ref, y_ref, o_ref,
                  xpad_ref, hs_ref, h_ref):
    ts = x_ref.shape[0]
    halo = SUBLANES
    groups = ts // SUBLANES

    @pl.when(pl.program_id(1) == 0)
    def _():
        xpad_ref[:, 0:halo, :] = jnp.zeros((LRU_BLOCKS, halo, LRU_BLOCK_W), F32)
        h_ref[...] = jnp.zeros(h_ref.shape, F32)

    @pl.when(pl.program_id(1) > 0)
    def _():
        xpad_ref[:, 0:halo, :] = xpad_ref[:, ts:ts + halo, :]

    row = lax.broadcasted_iota(jnp.int32, (groups, LRU_BLOCK_W), 0)

    for n in range(LRU_BLOCKS):
        cols = slice(n * LRU_BLOCK_W, (n + 1) * LRU_BLOCK_W)
        xpad_ref[n, halo:halo + ts, :] = x_ref[:, cols]
        phase = {p: xpad_ref[n, pl.ds(halo + p, groups, stride=SUBLANES), :]
                 for p in range(1 - CONV_WIDTH, SUBLANES)}
        xcs = []
        for p in range(SUBLANES):
            xc = cb_ref[:, cols]
            for j in range(CONV_WIDTH):
                xc = xc + cw_ref[j:j + 1, cols] * phase[p + j - (CONV_WIDTH - 1)]
            xcs.append(xc)
        xc = jnp.concatenate(xcs, axis=0)
        xb = xc.astype(BF16)
        r = _sigmoid(jnp.dot(xb, wa_ref[n], preferred_element_type=F32) + ba_ref[:, cols])
        gi = _sigmoid(jnp.dot(xb, wx_ref[n], preferred_element_type=F32) + bx_ref[:, cols])
        log_a = (-LRU_C * _softplus(-lam_ref[:, cols])) * r
        a = jnp.exp(log_a)
        b = jnp.sqrt(1.0 - a * a) * (gi * xc)

        a_run, b_run = a[0:groups], b[0:groups]
        a_runs, b_runs = [a_run], [b_run]
        for p in range(1, SUBLANES):
            a_p = a[p * groups:(p + 1) * groups]
            b_run = a_p * b_run + b[p * groups:(p + 1) * groups]
            a_run = a_p * a_run
            a_runs.append(a_run)
            b_runs.append(b_run)
        h0 = h_ref[:, cols]
        end = jnp.where(row == 0, b_run + a_run * h0, b_run)
        prod = a_run
        d = 1
        while d < groups:
            keep = row >= d
            end = jnp.where(keep, prod * pltpu.roll(end, d, axis=0) + end, end)
            prod = jnp.where(keep, prod * pltpu.roll(prod, d, axis=0), prod)
            d *= 2
        h_in = jnp.where(row == 0, h0, pltpu.roll(end, 1, axis=0))
        h_ref[:, cols] = end[groups - 1:groups, :]
        for p in range(SUBLANES):
            hs_ref[n, pl.ds(p, groups, stride=SUBLANES), :] = b_runs[p] + a_runs[p] * h_in
        o_ref[:, cols] = (hs_ref[n] * y_ref[:, cols].astype(F32)).astype(o_ref.dtype)


def _rglru(ub3, uf3, conv_w, conv_b, wa, ba, wx, bx, lam, layer):
    bn, s_len, _ = uf3.shape
    w = LRU_BLOCKS * LRU_BLOCK_W
    ts = min(LRU_TS, s_len)
    vec = lambda: pl.BlockSpec((None, 1, w), lambda b, t: (layer, 0, 0))
    blk = lambda: pl.BlockSpec((None, LRU_BLOCKS, LRU_BLOCK_W, LRU_BLOCK_W), lambda b, t: (layer, 0, 0, 0))
    return pl.pallas_call(
        _rglru_kernel,
        grid=(bn, s_len // ts),
        in_specs=[
            pl.BlockSpec((None, CONV_WIDTH, w), lambda b, t: (layer, 0, 0)),
            vec(), blk(), vec(), blk(), vec(), vec(),
            pl.BlockSpec((None, ts, w), lambda b, t: (b, t, COLF_LX // w)),
            pl.BlockSpec((None, ts, w), lambda b, t: (b, t, COLB_LY // w)),
        ],
        out_specs=pl.BlockSpec((None, ts, w), lambda b, t: (b, t, 0)),
        out_shape=jax.ShapeDtypeStruct((bn, s_len, w), BF16),
        scratch_shapes=[pltpu.VMEM((LRU_BLOCKS, ts + SUBLANES, LRU_BLOCK_W), F32),
                        pltpu.VMEM((LRU_BLOCKS, ts, LRU_BLOCK_W), F32), pltpu.VMEM((1, w), F32)],
        compiler_params=_params("parallel", "arbitrary"),
        name="rglru",
    )(conv_w, conv_b, wa, ba, wx, bx, lam, uf3, ub3)


def _merge_kernel(od_ref, or_ref, ol_ref, g0_ref, g1_ref, g2_ref, x_ref, wb_ref, wo_ref, o_ref):
    merged = (g0_ref[...].astype(F32) * jnp.dot(od_ref[...], wb_ref[0], preferred_element_type=F32)
              + g1_ref[...].astype(F32) * jnp.dot(or_ref[...], wb_ref[1], preferred_element_type=F32)
              + g2_ref[...].astype(F32) * jnp.dot(ol_ref[...], wb_ref[2], preferred_element_type=F32))
    o_ref[...] = x_ref[...] + jnp.dot(merged.astype(BF16), wo_ref[...], preferred_element_type=F32)


def _merge(od, o_r, o_l, u2, x2, w_branch, w_out, layer, tm=512):
    m, d = x2.shape
    tm = min(tm, m)
    row = lambda: pl.BlockSpec((tm, d), lambda i: (i, 0))
    gate = lambda k: pl.BlockSpec((tm, d), lambda i: (i, COLB_GATES // d + k))
    return pl.pallas_call(
        _merge_kernel,
        grid=(m // tm,),
        in_specs=[
            row(), row(), row(), gate(0), gate(1), gate(2), row(),
            pl.BlockSpec((None, 3, d, d), lambda i: (layer, 0, 0, 0)),
            pl.BlockSpec((None, d, d), lambda i: (layer, 0, 0)),
        ],
        out_specs=row(),
        out_shape=jax.ShapeDtypeStruct((m, d), F32),
        compiler_params=_params("parallel"),
        name="merge",
    )(od, o_r, o_l, u2, u2, u2, x2, w_branch, w_out)


def _xattn_kernel(x_ref, g_ref, wq_ref, kv_ref, wo_ref, o_ref):
    x = x_ref[...]
    h = _rms_rows(x, g_ref[...], 1e-6).astype(BF16)
    q = jnp.dot(h, wq_ref[...], preferred_element_type=F32).astype(BF16)
    heads = []
    for hd in range(XA_HEADS):
        kc = slice(hd * XA_HEAD_DIM, (hd + 1) * XA_HEAD_DIM)
        vc = slice(XA_WIDTH + hd * XA_HEAD_DIM, XA_WIDTH + (hd + 1) * XA_HEAD_DIM)
        s = lax.dot_general(q[:, kc], kv_ref[:, kc], (((1,), (1,)), ((), ())),
                            preferred_element_type=F32) * (XA_HEAD_DIM ** -0.5)
        e = jnp.exp(s - jnp.max(s, axis=-1, keepdims=True))
        pv = jnp.dot(e.astype(BF16), kv_ref[:, vc], preferred_element_type=F32)
        heads.append((pv / jnp.sum(e, axis=-1, keepdims=True)).astype(BF16))
    o = jnp.concatenate(heads, axis=-1)
    o_ref[...] = x + jnp.dot(o, wo_ref[...], preferred_element_type=F32)


def _xattn(x3, g, wq, kv3, wo, layer, tm=1024):
    bn, s_len, d = x3.shape
    tm = min(tm, s_len)
    m_len = kv3.shape[1]
    return pl.pallas_call(
        _xattn_kernel,
        grid=(bn, s_len // tm),
        in_specs=[
            pl.BlockSpec((None, tm, d), lambda b, i: (b, i, 0)),
            pl.BlockSpec((None, 1, d), lambda b, i: (layer, 0, 0)),
            pl.BlockSpec((None, d, XA_WIDTH), lambda b, i: (layer, 0, 0)),
            pl.BlockSpec((None, m_len, 2 * XA_WIDTH), lambda b, i: (b, 0, 0)),
            pl.BlockSpec((None, XA_WIDTH, d), lambda b, i: (layer, 0, 0)),
        ],
        out_specs=pl.BlockSpec((None, tm, d), lambda b, i: (b, i, 0)),
        out_shape=jax.ShapeDtypeStruct((bn, s_len, d), F32),
        compiler_params=_params("parallel", "parallel"),
        name="xattn",
    )(x3, g, wq, kv3, wo)


def _mlp_kernel(x_ref, g_ref, w1_ref, w2_ref, gf_ref, o_ref, *, final_norm):
    x = x_ref[...]
    h = _rms_rows(x, g_ref[...], 1e-6).astype(BF16)
    d = x.shape[1]
    acc = x
    for c in range(w1_ref.shape[1] // d):
        cols = slice(c * d, (c + 1) * d)
        a = jnp.maximum(jnp.dot(h, w1_ref[:, cols], preferred_element_type=F32), 0.0)
        acc = acc + jnp.dot((a * a).astype(BF16), w2_ref[cols, :], preferred_element_type=F32)
    o_ref[...] = _rms_rows(acc, gf_ref[...], 1e-6) if final_norm else acc


def _mlp(x2, g, w1, w2, g_final, layer, final_norm, tm=512):
    m, d = x2.shape
    tm = min(tm, m)
    dff = w1.shape[2]
    return pl.pallas_call(
        functools.partial(_mlp_kernel, final_norm=final_norm),
        grid=(m // tm,),
        in_specs=[
            pl.BlockSpec((tm, d), lambda i: (i, 0)),
            pl.BlockSpec((None, 1, d), lambda i: (layer, 0, 0)),
            pl.BlockSpec((None, d, dff), lambda i: (layer, 0, 0)),
            pl.BlockSpec((None, dff, d), lambda i: (layer, 0, 0)),
            pl.BlockSpec((1, d), lambda i: (0, 0)),
        ],
        out_specs=pl.BlockSpec((tm, d), lambda i: (i, 0)),
        out_shape=jax.ShapeDtypeStruct((m, d), F32),
        compiler_params=_params("parallel"),
        name="mlp",
    )(x2, g, w1, w2, g_final)


def kernel(x, mem, norm_mix, w_in, diff_lq1, diff_lk1, diff_lq2, diff_lk2, diff_subln, lru_conv_w, lru_conv_b, lru_wa, lru_ba, lru_wx, lru_bx, lru_lambda, w_branch, w_out, norm_xattn, norm_mem, xa_wq, xa_wkv, xa_wo, norm_mlp, mlp_w1, mlp_w2, norm_final):
    bn, s_len, d = x.shape
    m_len = mem.shape[1]
    depth = norm_mix.shape[0]
    rows = bn * s_len

    qk0, qk1 = 3 * PROJ_TN, 4 * PROJ_TN
    w_qk = w_in[:, :, qk0:qk1].reshape(depth, d, (qk1 - qk0) // RET_KEY_DIM, RET_KEY_DIM // 2, 2)
    w_qk = jnp.swapaxes(w_qk, -1, -2).reshape(depth, d, qk1 - qk0)
    w_in_b = lax.dynamic_update_slice(w_in.astype(BF16), w_qk.astype(BF16), (0, 0, qk0))
    w_branch_b, w_out_b = w_branch.astype(BF16), w_out.astype(BF16)
    wq_b, wkv_b, wo_b = xa_wq.astype(BF16), xa_wkv.astype(BF16), xa_wo.astype(BF16)
    w1_b, w2_b = mlp_w1.astype(BF16), mlp_w2.astype(BF16)
    wa_b, wx_b = lru_wa.astype(BF16), lru_wx.astype(BF16)
    row3 = lambda p: p.reshape(depth, 1, p.shape[-1])
    subln_lanes = jnp.broadcast_to(diff_subln[:, :, None], (depth, DIFF_VAL_DIM, ATT_TQ))
    conv_w = lru_conv_w.reshape(depth, CONV_WIDTH, d)

    pos = jnp.arange(s_len, dtype=F32)
    angle = 1.0 / (10000.0 ** jnp.linspace(0.0, 1.0, RET_KEY_DIM // 2, dtype=F32))
    phase = pos[:, None] * angle[None, :]
    cos = jnp.concatenate([jnp.cos(phase), jnp.cos(phase)], axis=1)
    sin = jnp.concatenate([-jnp.sin(phase), jnp.sin(phase)], axis=1)
    log_g = jnp.log(1.0 - jnp.exp2(-5.0 - jnp.arange(RET_HEADS, dtype=F32)))
    lg = jnp.broadcast_to(log_g[:, None, None], (RET_HEADS, 1, RET_VAL_DIM))
    lam_inits = [0.8 - 0.6 * math.exp(-0.3 * l) for l in range(depth)]
    cst = jnp.zeros((depth, 1, 128), F32)
    cst = cst.at[:, 0, 0].set(jnp.asarray(lam_inits, F32))
    cst = cst.at[:, 0, 1].set(jnp.asarray([1.0 - v for v in lam_inits], F32))

    x2 = x.reshape(rows, d)
    mem2 = mem.reshape(bn * m_len, d)
    for l in range(depth):
        ub2, uf2 = _proj_in(x2, row3(norm_mix), w_in_b, l)
        ub3 = ub2.reshape(bn, s_len, ub2.shape[1])
        uf3 = uf2.reshape(bn, s_len, uf2.shape[1])
        od = _diff_attn(ub3, row3(diff_lq1), row3(diff_lk1), row3(diff_lq2), row3(diff_lk2), cst,
                        subln_lanes, l)
        o_r = _retention(ub3, uf3, lg, cos, sin)
        o_l = _rglru(ub3, uf3, conv_w, row3(lru_conv_b), wa_b, row3(lru_ba), wx_b, row3(lru_bx),
                     row3(lru_lambda), l)
        x2 = _merge(od.reshape(rows, d), o_r.reshape(rows, d), o_l.reshape(rows, d), ub2, x2,
                    w_branch_b, w_out_b, l)
        kv = _norm_matmul(mem2, row3(norm_mem), wkv_b, l, 2 * XA_WIDTH, BF16, tm=512, tn=2 * XA_WIDTH)
        x2 = _xattn(x2.reshape(bn, s_len, d), row3(norm_xattn), wq_b, kv.reshape(bn, m_len, 2 * XA_WIDTH),
                    wo_b, l).reshape(rows, d)
        x2 = _mlp(x2, row3(norm_mlp), w1_b, w2_b, norm_final.reshape(1, d), l, final_norm=(l == depth - 1))
    return x2.reshape(bn, s_len, d)
```
